```python
import math
import jax, jax.numpy as jnp
from jax import lax
import numpy as np

D_MODEL = 1024
BATCH = 8
SEQ = 4096
DEPTH = 1

GRID_W = 64
CTX_LEN = 256
EPS = 1e-6
MIX_WIDTH = D_MODEL
MLA_HEADS = 8
MLA_NOPE = 64
MLA_ROPE = 32
MLA_V = 64
MLA_Q_RANK = 256
MLA_KV_RANK = 128
MLA_WIDTH = MLA_HEADS * MLA_V
LRU_WIDTH = MIX_WIDTH - MLA_WIDTH
LRU_BLOCKS = 8
LRU_BLOCK = LRU_WIDTH // LRU_BLOCKS
CONV_W = 4
CONV_LEFT = 2
RG_C = 8.0
ROPE_BASE = 10000.0
ROPE_AXIS = MLA_ROPE // 2
ATTN_SCALE = (MLA_NOPE + MLA_ROPE) ** -0.5
Q_BLOCK = 128
COL_Q = 0
COL_KV = COL_Q + MLA_Q_RANK
COL_KR = COL_KV + MLA_KV_RANK
COL_LRU_X = COL_KR + MLA_ROPE
COL_LRU_G = COL_LRU_X + LRU_WIDTH
IN_COLS = COL_LRU_G + LRU_WIDTH
N_EXPERTS = 64
TOP_K = 8
N_GROUPS = 8
TOPK_GROUPS = 4
EXPERT_FF = 256
SHARED_FF = 256
ROUTED_SCALE = 2.5
DISPATCH_BLOCK = 256

kernel_name = "hymba_mla_rglru_moe_dit_layer"


def rmsnorm(t, g):
    t32 = t.astype(jnp.float32)
    t32 = t32 * lax.rsqrt(jnp.mean(t32 * t32, axis=-1, keepdims=True) + EPS)
    return (t32 * g.astype(jnp.float32)).astype(t.dtype)


def modulate(t, shift, scale):
    return t * (1.0 + scale) + shift


def axial_rope_tables(n_tokens):
    rows = n_tokens // GRID_W
    row = jnp.repeat(jnp.arange(rows, dtype=jnp.float32), GRID_W)
    col = jnp.tile(jnp.arange(GRID_W, dtype=jnp.float32), rows)
    inv_freq = ROPE_BASE ** (-jnp.arange(0, ROPE_AXIS, 2, dtype=jnp.float32) / ROPE_AXIS)
    ang = jnp.stack([row, col], axis=-1)[:, :, None] * inv_freq
    return jnp.cos(ang)[:, None], jnp.sin(ang)[:, None]


def apply_axial_rope(t, cos, sin):
    b_, s_, h_, _ = t.shape
    t32 = t.astype(jnp.float32).reshape(b_, s_, h_, 2, 2, ROPE_AXIS // 2)
    t1, t2 = t32[..., 0, :], t32[..., 1, :]
    out = jnp.stack([t1 * cos - t2 * sin, t1 * sin + t2 * cos], axis=-2)
    return out.reshape(b_, s_, h_, MLA_ROPE).astype(t.dtype)


def mla_q(proj, q_norm_g, w_q_up):
    b_, l_, _ = proj.shape
    q = (rmsnorm(proj[..., COL_Q:COL_KV], q_norm_g) @ w_q_up).reshape(b_, l_, MLA_HEADS, MLA_NOPE + MLA_ROPE)
    return q[..., :MLA_NOPE], q[..., MLA_NOPE:]


def mla_kv(proj, kv_norm_g, w_kv_up):
    b_, l_, _ = proj.shape
    kv = (rmsnorm(proj[..., COL_KV:COL_KR], kv_norm_g) @ w_kv_up).reshape(b_, l_, MLA_HEADS, MLA_NOPE + MLA_V)
    k_rope = proj[..., COL_KR:COL_LRU_X][:, :, None, :]
    return kv[..., :MLA_NOPE], k_rope, kv[..., MLA_NOPE:]


def join_key(k_nope, k_rope):
    return jnp.concatenate([k_nope, jnp.broadcast_to(k_rope, k_nope.shape[:-1] + (MLA_ROPE,))], axis=-1)


def attend(q, k, v):
    s = jnp.einsum('bqhd,bkhd->bhqk', q, k, preferred_element_type=jnp.float32) * ATTN_SCALE
    p = jax.nn.softmax(s, axis=-1)
    return jnp.einsum('bhqk,bkhd->bqhd', p.astype(v.dtype), v)


def short_conv(u, w, b):
    l_ = u.shape[1]
    up = jnp.pad(u, ((0, 0), (CONV_LEFT, CONV_W - 1 - CONV_LEFT), (0, 0)))
    out = b
    for j in range(CONV_W):
        out = out + up[:, j:j + l_] * w[j]
    return out


def block_diag(u, w, b):
    ub = u.reshape(u.shape[:-1] + (LRU_BLOCKS, LRU_BLOCK))
    return jnp.einsum('blnc,ncd->blnd', ub, w).reshape(u.shape) + b


def rglru_coeffs(u, w_a, b_a, w_x, b_x, lam):
    r = jax.nn.sigmoid(block_diag(u, w_a, b_a).astype(jnp.float32))
    i = jax.nn.sigmoid(block_diag(u, w_x, b_x).astype(jnp.float32))
    log_a = -RG_C * r * jax.nn.softplus(-lam.astype(jnp.float32))
    a = jnp.exp(log_a)
    b = jnp.sqrt(-jnp.expm1(2.0 * log_a)) * (i * u.astype(jnp.float32))
    return a, b


def linear_scan(a, b, h0):
    b = b.at[:, 0].add(a[:, 0] * h0)

    def combine(left, right):
        return left[0] * right[0], right[0] * left[1] + right[1]

    return lax.associative_scan(combine, (a, b), axis=1)[1]


def rglru_bidirectional(u_lat, u_ctx, lru_w_a, lru_b_a, lru_w_x, lru_b_x, lru_lambda, need_ctx):
    ys_lat, ys_ctx = [], []
    for d in range(2):
        flip = (lambda t: t) if d == 0 else (lambda t: jnp.flip(t, axis=1))
        params = (lru_w_a[d], lru_b_a[d], lru_w_x[d], lru_b_x[d], lru_lambda[d])
        a_c, b_c = rglru_coeffs(flip(u_ctx), *params)
        h_c = linear_scan(a_c, b_c, jnp.zeros_like(b_c[:, 0]))
        a_l, b_l = rglru_coeffs(flip(u_lat), *params)
        h_l = linear_scan(a_l, b_l, h_c[:, -1])
        ys_lat.append(flip(h_l))
        if need_ctx:
            ys_ctx.append(flip(h_c))
    y_ctx = ys_ctx[0] + ys_ctx[1] if need_ctx else None
    return ys_lat[0] + ys_lat[1], y_ctx


def hybrid_mixer(h_lat, h_ctx, cos, sin, w_in, q_norm_g, w_q_up, kv_norm_g, w_kv_up,
                 conv_w, conv_b, lru_w_a, lru_b_a, lru_w_x, lru_b_x, lru_lambda, need_ctx):
    b_, s_, _ = h_lat.shape
    p_lat = h_lat @ w_in
    p_ctx = h_ctx @ w_in

    qn_l, qr_l = mla_q(p_lat, q_norm_g, w_q_up)
    kn_l, kr_l, v_l = mla_kv(p_lat, kv_norm_g, w_kv_up)
    kn_c, kr_c, v_c = mla_kv(p_ctx, kv_norm_g, w_kv_up)
    q_lat = jnp.concatenate([qn_l, apply_axial_rope(qr_l, cos, sin)], axis=-1)
    k_ctx = join_key(kn_c, kr_c)
    k_all = jnp.concatenate([join_key(kn_l, apply_axial_rope(kr_l, cos, sin)), k_ctx], axis=1)
    v_all = jnp.concatenate([v_l, v_c], axis=1)
    n_blk = s_ // Q_BLOCK
    q_blocks = jnp.moveaxis(q_lat.reshape(b_, n_blk, Q_BLOCK, MLA_HEADS, MLA_NOPE + MLA_ROPE), 1, 0)
    o_lat = lax.map(lambda qb: attend(qb, k_all, v_all), q_blocks)
    o_lat = jnp.moveaxis(o_lat, 0, 1).reshape(b_, s_, MLA_WIDTH)

    u_lat = short_conv(p_lat[..., COL_LRU_X:COL_LRU_G], conv_w, conv_b)
    u_ctx = short_conv(p_ctx[..., COL_LRU_X:COL_LRU_G], conv_w, conv_b)
    r_lat, r_ctx = rglru_bidirectional(u_lat, u_ctx, lru_w_a, lru_b_a, lru_w_x, lru_b_x, lru_lambda, need_ctx)
    g_lat = jax.nn.gelu(p_lat[..., COL_LRU_G:].astype(jnp.float32))
    y_lat = jnp.concatenate([o_lat, (r_lat * g_lat).astype(h_lat.dtype)], axis=-1)

    y_ctx = None
    if need_ctx:
        qn_c, qr_c = mla_q(p_ctx, q_norm_g, w_q_up)
        o_ctx = attend(jnp.concatenate([qn_c, qr_c], axis=-1), k_ctx, v_c).reshape(b_, -1, MLA_WIDTH)
        g_ctx = jax.nn.gelu(p_ctx[..., COL_LRU_G:].astype(jnp.float32))
        y_ctx = jnp.concatenate([o_ctx, (r_ctx * g_ctx).astype(h_ctx.dtype)], axis=-1)
    return y_lat, y_ctx


def swiglu(u, w_gate, w_up, w_down):
    return (jax.nn.silu(u @ w_gate) * (u @ w_up)) @ w_down


def routed_experts(h, eidx, w, exp_w_gate, exp_w_up, exp_w_down):
    t_, _ = h.shape
    n_assign = t_ * TOP_K
    n_blocks = n_assign // DISPATCH_BLOCK + N_EXPERTS
    e_flat = eidx.reshape(-1)
    tok_flat = jnp.repeat(jnp.arange(t_, dtype=jnp.int32), TOP_K)
    order = jnp.argsort(e_flat, stable=True)
    e_sorted = e_flat[order]
    counts = jnp.bincount(e_flat, length=N_EXPERTS)
    starts = jnp.cumsum(counts) - counts
    padded = (counts + DISPATCH_BLOCK - 1) // DISPATCH_BLOCK * DISPATCH_BLOCK
    pad_end = jnp.cumsum(padded)
    dest = pad_end[e_sorted] - padded[e_sorted] + (jnp.arange(n_assign, dtype=jnp.int32) - starts[e_sorted])
    slot_tok = jnp.zeros((n_blocks * DISPATCH_BLOCK,), jnp.int32).at[dest].set(tok_flat[order])
    slot_w = jnp.zeros((n_blocks * DISPATCH_BLOCK,), w.dtype).at[dest].set(w.reshape(-1)[order])
    blk_expert = jnp.minimum(
        jnp.searchsorted(pad_end, jnp.arange(n_blocks, dtype=jnp.int32) * DISPATCH_BLOCK, side='right'),
        N_EXPERTS - 1)

    def body(acc, blk):
        tok_b, w_b, e_b = blk
        y = swiglu(h[tok_b], exp_w_gate[e_b], exp_w_up[e_b], exp_w_down[e_b])
        return acc.at[tok_b].add(y * w_b[:, None]), None

    acc, _ = lax.scan(body, jnp.zeros_like(h),
                      (slot_tok.reshape(n_blocks, DISPATCH_BLOCK),
                       slot_w.reshape(n_blocks, DISPATCH_BLOCK), blk_expert))
    return acc


def moe_ffn(h, router_w, router_bias, exp_w_gate, exp_w_up, exp_w_down, sh_w_gate, sh_w_up, sh_w_down):
    t_, _ = h.shape
    scores = jax.nn.sigmoid(jnp.matmul(h, router_w, preferred_element_type=jnp.float32))
    sel = scores + router_bias.astype(jnp.float32)
    grp = sel.reshape(t_, N_GROUPS, N_EXPERTS // N_GROUPS)
    grp_score = lax.top_k(grp, 2)[0].sum(-1)
    _, gidx = lax.top_k(grp_score, TOPK_GROUPS)
    gmask = jax.nn.one_hot(gidx, N_GROUPS, dtype=jnp.float32).sum(1) > 0
    emask = jnp.repeat(gmask, N_EXPERTS // N_GROUPS, axis=1)
    _, eidx = lax.top_k(jnp.where(emask, sel, -jnp.inf), TOP_K)
    w = jnp.take_along_axis(scores, eidx, axis=1)
    w = ROUTED_SCALE * w / w.sum(-1, keepdims=True)
    routed = routed_experts(h, eidx, w.astype(h.dtype), exp_w_gate, exp_w_up, exp_w_down)
    return routed + swiglu(h, sh_w_gate, sh_w_up, sh_w_down)


def setup_inputs(seed: int = 0) -> dict:
    key = jax.random.key(seed)
    k = jax.random.split(key, 32)
    L, D = DEPTH, D_MODEL
    f32 = jnp.float32

    def dense(kk, shape, fan_in, gain=1.0):
        return (gain * fan_in ** -0.5) * jax.random.normal(kk, shape, f32)

    def norm_gain(kk, shape):
        return 1.0 + 0.05 * jax.random.normal(kk, shape, f32)

    def small(kk, shape, s=0.02):
        return s * jax.random.normal(kk, shape, f32)

    u = jax.random.uniform(k[19], (L, 2, LRU_WIDTH), f32, 0.9, 0.999)
    a_base = u ** (1.0 / RG_C)
    lru_lambda = jnp.log(a_base) - jnp.log1p(-a_base)
    return {
        "x": jax.random.normal(k[0], (BATCH, SEQ, D), f32),
        "c": jax.random.normal(k[1], (BATCH, D), f32),
        "ctx": jax.random.normal(k[2], (BATCH, CTX_LEN, D), f32),
        "c_ctx": jax.random.normal(k[3], (D,), f32),
        "w_mod": dense(k[4], (L, D, 6 * D), D, 0.5),
        "b_mod": small(k[5], (L, 6 * D)),
        "norm_mix_g": norm_gain(k[6], (L, D)),
        "w_in": dense(k[7], (L, D, IN_COLS), D),
        "q_norm_g": norm_gain(k[8], (L, MLA_Q_RANK)),
        "w_q_up": dense(k[9], (L, MLA_Q_RANK, MLA_HEADS * (MLA_NOPE + MLA_ROPE)), MLA_Q_RANK),
        "kv_norm_g": norm_gain(k[10], (L, MLA_KV_RANK)),
        "w_kv_up": dense(k[11], (L, MLA_KV_RANK, MLA_HEADS * (MLA_NOPE + MLA_V)), MLA_KV_RANK),
        "conv_w": dense(k[12], (L, CONV_W, LRU_WIDTH), CONV_W),
        "conv_b": small(k[13], (L, LRU_WIDTH)),
        "lru_w_a": dense(k[14], (L, 2, LRU_BLOCKS, LRU_BLOCK, LRU_BLOCK), LRU_BLOCK),
        "lru_b_a": small(k[15], (L, 2, LRU_WIDTH)),
        "lru_w_x": dense(k[16], (L, 2, LRU_BLOCKS, LRU_BLOCK, LRU_BLOCK), LRU_BLOCK),
        "lru_b_x": small(k[17], (L, 2, LRU_WIDTH)),
        "lru_lambda": lru_lambda,
        "w_out": dense(k[18], (L, MIX_WIDTH, D), MIX_WIDTH),
        "norm_ffn_g": norm_gain(k[20], (L, D)),
        "router_w": dense(k[21], (L, D, N_EXPERTS), D),
        "router_bias": small(k[22], (L, N_EXPERTS), 0.01),
        "exp_w_gate": dense(k[23], (L, N_EXPERTS, D, EXPERT_FF), D),
        "exp_w_up": dense(k[24], (L, N_EXPERTS, D, EXPERT_FF), D),
        "exp_w_down": dense(k[25], (L, N_EXPERTS, EXPERT_FF, D), EXPERT_FF),
        "sh_w_gate": dense(k[26], (L, D, SHARED_FF), D),
        "sh_w_up": dense(k[27], (L, D, SHARED_FF), D),
        "sh_w_down": dense(k[28], (L, SHARED_FF, D), SHARED_FF),
        "final_norm_g": norm_gain(k[29], (D,)),
    }


def reference(x, c, ctx, c_ctx, w_mod, b_mod, norm_mix_g, w_in, q_norm_g, w_q_up, kv_norm_g, w_kv_up,
              conv_w, conv_b, lru_w_a, lru_b_a, lru_w_x, lru_b_x, lru_lambda, w_out, norm_ffn_g,
              router_w, router_bias, exp_w_gate, exp_w_up, exp_w_down, sh_w_gate, sh_w_up, sh_w_down,
              final_norm_g):
    b_, s_, d_ = x.shape
    n_ctx = ctx.shape[1]
    cos, sin = axial_rope_tables(s_)
    for l in range(DEPTH):
        last = l == DEPTH - 1
        m_lat = jax.nn.silu(c) @ w_mod[l] + b_mod[l]
        m_ctx = jax.nn.silu(c_ctx) @ w_mod[l] + b_mod[l]
        sh1, sc1, g1, sh2, sc2, g2 = jnp.split(m_lat[:, None, :], 6, axis=-1)
        csh1, csc1, cg1, csh2, csc2, cg2 = jnp.split(m_ctx, 6)

        h_lat = modulate(rmsnorm(x, norm_mix_g[l]), sh1, sc1)
        h_ctx = modulate(rmsnorm(ctx, norm_mix_g[l]), csh1, csc1)
        y_lat, y_ctx = hybrid_mixer(h_lat, h_ctx, cos, sin, w_in[l], q_norm_g[l], w_q_up[l], kv_norm_g[l],
                                    w_kv_up[l], conv_w[l], conv_b[l], lru_w_a[l], lru_b_a[l], lru_w_x[l],
                                    lru_b_x[l], lru_lambda[l], not last)
        x = x + g1 * (y_lat @ w_out[l])
        if not last:
            ctx = ctx + cg1 * (y_ctx @ w_out[l])

        f_lat = modulate(rmsnorm(x, norm_ffn_g[l]), sh2, sc2).reshape(-1, d_)
        moe_args = (router_w[l], router_bias[l], exp_w_gate[l], exp_w_up[l], exp_w_down[l],
                    sh_w_gate[l], sh_w_up[l], sh_w_down[l])
        if last:
            x = x + g2 * moe_ffn(f_lat, *moe_args).reshape(b_, s_, d_)
        else:
            f_ctx = modulate(rmsnorm(ctx, norm_ffn_g[l]), csh2, csc2).reshape(-1, d_)
            out = moe_ffn(jnp.concatenate([f_lat, f_ctx], axis=0), *moe_args)
            x = x + g2 * out[:b_ * s_].reshape(b_, s_, d_)
            ctx = ctx + cg2 * out[b_ * s_:].reshape(b_, n_ctx, d_)
    return rmsnorm(x, final_norm_g)
```

```python
import functools

import jax
import jax.numpy as jnp
from jax import lax
from jax.experimental import pallas as pl
from jax.experimental.pallas import tpu as pltpu

GRID_W = 64
EPS = 1e-6
MLA_HEADS = 8
MLA_NOPE = 64
MLA_ROPE = 32
MLA_V = 64
MLA_Q_RANK = 256
MLA_KV_RANK = 128
LRU_WIDTH = 512
LRU_BLOCKS = 8
CONV_W = 4
RG_C = 8.0
ROPE_BASE = 10000.0
ROPE_AXIS = MLA_ROPE // 2
ATTN_SCALE = (MLA_NOPE + MLA_ROPE) ** -0.5
N_EXPERTS = 64
TOP_K = 8
N_GROUPS = 8
TOPK_GROUPS = 4
EXPERT_FF = 256
ROUTED_SCALE = 2.5
COL_KV = MLA_Q_RANK
COL_KR = COL_KV + MLA_KV_RANK
COL_LRU_X = COL_KR + MLA_ROPE
COL_LRU_G = COL_LRU_X + LRU_WIDTH

LANES = 128
SUBLANES = 8
VMEM_LIMIT = 56 * 1024 * 1024

TOKEN_TILE = 512
Q_TILE = 256
SCAN_CHUNK = 128
EXPERT_TILE = 256
TILE_STRIDE = EXPERT_TILE + 1
SCATTER_UNROLL = 4

F32 = jnp.float32
BF16 = jnp.bfloat16
NEG_INF = float("-inf")


def _params(*sem):
    return pltpu.CompilerParams(dimension_semantics=sem, vmem_limit_bytes=VMEM_LIMIT)


def _rms(t, g):
    return t * lax.rsqrt(jnp.mean(t * t, axis=-1, keepdims=True) + EPS) * g


def _gelu_tanh(t):
    return 0.5 * t * (1.0 + jnp.tanh(0.7978845608028654 * (t + 0.044715 * (t * t * t))))


def _silu(t):
    return t * jax.nn.sigmoid(t)


def _nt_dot(a, b):
    return lax.dot_general(a, b, (((1,), (1,)), ((), ())), preferred_element_type=F32)


def _mod_kernel(c_ref, w_ref, b_ref, o_ref):
    s = _silu(c_ref[...])
    o_ref[...] = jnp.dot(s, w_ref[...], precision=lax.Precision.HIGHEST,
                         preferred_element_type=F32) + b_ref[...]


def _modulation(cc, w, b):
    rows, d = cc.shape
    n = w.shape[1]
    tn = 1024
    return pl.pallas_call(
        _mod_kernel,
        out_shape=jax.ShapeDtypeStruct((rows, n), F32),
        grid=(n // tn,),
        in_specs=[pl.BlockSpec((rows, d), lambda j: (0, 0)),
                  pl.BlockSpec((d, tn), lambda j: (0, j)),
                  pl.BlockSpec((1, tn), lambda j: (0, j))],
        out_specs=pl.BlockSpec((rows, tn), lambda j: (0, j)),
        compiler_params=_params("arbitrary"),
        name="modulation",
    )(cc, w, b)


def _inproj_kernel(*refs, with_q):
    (x_ref, sh_ref, sc_ref, g_ref, win_ref, qg_ref, kvg_ref, wq_ref, wqs_ref, wk_ref, wks_ref,
     wv_ref, cq_ref, sq_ref, ck_ref, sk_ref) = refs[:16]
    if with_q:
        q_ref, k_ref, v_ref, lx_ref, lg_ref = refs[16:]
    else:
        k_ref, v_ref, lx_ref, lg_ref = refs[16:]
    h = _rms(x_ref[0], g_ref[...]) * (1.0 + sc_ref[0]) + sh_ref[0]
    p = jnp.dot(h.astype(BF16), win_ref[...], preferred_element_type=F32)

    if with_q:
        qn = _rms(p[:, :MLA_Q_RANK], qg_ref[...]).astype(BF16)
        qa = jnp.dot(qn, wq_ref[...], preferred_element_type=F32)
        qb = jnp.dot(qn, wqs_ref[...], preferred_element_type=F32)
        cq, sq = cq_ref[...], sq_ref[...]
        for hd in range(MLA_HEADS):
            sl = slice(hd * LANES, (hd + 1) * LANES)
            q_ref[0, :, sl] = (qa[:, sl] * cq + qb[:, sl] * sq).astype(BF16)

    kvn = _rms(p[:, COL_KV:COL_KR], kvg_ref[...])
    kvn16 = kvn.astype(BF16)
    kin = jnp.concatenate([kvn16, p[:, COL_KR:COL_KR + LANES].astype(BF16)], axis=-1)
    ka = jnp.dot(kin, wk_ref[...], preferred_element_type=F32)
    kb = jnp.dot(kin, wks_ref[...], preferred_element_type=F32)
    ck, sk = ck_ref[...], sk_ref[...]
    for hd in range(MLA_HEADS):
        sl = slice(hd * LANES, (hd + 1) * LANES)
        k_ref[0, :, sl] = (ka[:, sl] * ck + kb[:, sl] * sk).astype(BF16)
    v_ref[0] = jnp.dot(kvn16, wv_ref[...], preferred_element_type=F32).astype(BF16)
    lx_ref[0] = p[:, 512:512 + LRU_WIDTH]
    lg_ref[0] = _gelu_tanh(p[:, 1024:1024 + LRU_WIDTH]).astype(BF16)


def _input_projection(x, sh, sc, g, wts, tabs, with_q):
    b_, s_, d = x.shape
    tm = min(TOKEN_TILE, s_)
    hw = MLA_HEADS * LANES
    per_batch = sh.shape[0] == b_
    mod_spec = pl.BlockSpec((1, 1, d), (lambda b, i: (b, 0, 0)) if per_batch else (lambda b, i: (0, 0, 0)))

    def const(a):
        return pl.BlockSpec(a.shape, lambda b, i: (0,) * a.ndim)

    tab_spec = pl.BlockSpec((tm, LANES), lambda b, i: (i, 0))
    wide = lambda w: pl.BlockSpec((1, tm, w), lambda b, i: (b, i, 0))
    weights = (wts["w_in"], wts["q_g"], wts["kv_g"], wts["wq"], wts["wq_sw"], wts["wk"], wts["wk_sw"], wts["wv"])
    out_shape = [jax.ShapeDtypeStruct((b_, s_, hw), BF16),
                 jax.ShapeDtypeStruct((b_, s_, hw), BF16),
                 jax.ShapeDtypeStruct((b_, s_, LRU_WIDTH), F32),
                 jax.ShapeDtypeStruct((b_, s_, LRU_WIDTH), BF16)]
    out_specs = [wide(hw), wide(hw), wide(LRU_WIDTH), wide(LRU_WIDTH)]
    if with_q:
        out_shape = [jax.ShapeDtypeStruct((b_, s_, hw), BF16)] + out_shape
        out_specs = [wide(hw)] + out_specs
    return pl.pallas_call(
        functools.partial(_inproj_kernel, with_q=with_q),
        out_shape=out_shape,
        grid=(b_, s_ // tm),
        in_specs=[wide(d), mod_spec, mod_spec, const(g)] + [const(w) for w in weights] + [tab_spec] * 4,
        out_specs=out_specs,
        compiler_params=_params("parallel", "parallel"),
        name="input_projection_lat" if with_q else "input_projection_ctx",
    )(x, sh, sc, g, *weights, *tabs)


def _attn_kernel(q_ref, kl_ref, kc_ref, vl_ref, vc_ref, o_ref):
    acc = None
    for a in range(2):
        sl = slice(a * LANES, (a + 1) * LANES)
        q = q_ref[0, :, sl]
        s1 = _nt_dot(q, kl_ref[0, :, sl])
        s2 = _nt_dot(q, kc_ref[0, :, sl])
        m = jnp.maximum(jnp.max(s1, axis=-1, keepdims=True), jnp.max(s2, axis=-1, keepdims=True))
        p1 = jnp.exp(s1 - m)
        p2 = jnp.exp(s2 - m)
        l = jnp.sum(p1, axis=-1, keepdims=True) + jnp.sum(p2, axis=-1, keepdims=True)
        o = (jnp.dot(p1.astype(BF16), vl_ref[0, :, sl], preferred_element_type=F32)
             + jnp.dot(p2.astype(BF16), vc_ref[0, :, sl], preferred_element_type=F32))
        o = o * (1.0 / l)
        acc = o if acc is None else acc + o
    o_ref[0] = acc.astype(BF16)


def _attention(q, k_lat, k_ctx, v_lat, v_ctx):
    b_, s_, hw = q.shape
    nc = k_ctx.shape[1]
    tq = min(Q_TILE, s_)
    pair = 2 * LANES
    n_pairs = hw // pair
    return pl.pallas_call(
        _attn_kernel,
        out_shape=jax.ShapeDtypeStruct((b_, s_, n_pairs * LANES), BF16),
        grid=(b_, n_pairs, s_ // tq),
        in_specs=[pl.BlockSpec((1, tq, pair), lambda b, j, i: (b, i, j)),
                  pl.BlockSpec((1, s_, pair), lambda b, j, i: (b, 0, j)),
                  pl.BlockSpec((1, nc, pair), lambda b, j, i: (b, 0, j)),
                  pl.BlockSpec((1, s_, pair), lambda b, j, i: (b, 0, j)),
                  pl.BlockSpec((1, nc, pair), lambda b, j, i: (b, 0, j))],
        out_specs=pl.BlockSpec((1, tq, LANES), lambda b, j, i: (b, i, j)),
        compiler_params=_params("parallel", "parallel", "arbitrary"),
        name="attention",
    )(q, k_lat, k_ctx, v_lat, v_ctx)


def _scan_chunk(a, b, reverse):
    tc = a.shape[0]
    row = lax.broadcasted_iota(jnp.int32, a.shape, 0)
    s = 1
    while s < tc:
        if s < SUBLANES:
            if reverse:
                keep = row < tc - s
                a_sh = jnp.where(keep, pltpu.roll(a, tc - s, 0), 1.0)
                b_sh = jnp.where(keep, pltpu.roll(b, tc - s, 0), 0.0)
            else:
                keep = row >= s
                a_sh = jnp.where(keep, pltpu.roll(a, s, 0), 1.0)
                b_sh = jnp.where(keep, pltpu.roll(b, s, 0), 0.0)
        else:
            ones = jnp.ones((s, a.shape[1]), F32)
            zeros = jnp.zeros((s, a.shape[1]), F32)
            if reverse:
                a_sh = jnp.concatenate([a[s:], ones], axis=0)
                b_sh = jnp.concatenate([b[s:], zeros], axis=0)
            else:
                a_sh = jnp.concatenate([ones, a[:tc - s]], axis=0)
                b_sh = jnp.concatenate([zeros, b[:tc - s]], axis=0)
        b = a * b_sh + b
        a = a * a_sh
        s *= 2
    return a, b


def _lru_kernel(xl_ref, xc_ref, gl_ref, cw_ref, cb_ref, wg_ref, bg_ref, lam_ref, o_ref,
                padl_ref, padc_ref, hf_ref):
    s_ = xl_ref.shape[1]
    nc = xc_ref.shape[1]
    tc = SCAN_CHUNK
    halo = SUBLANES
    zero_halo = jnp.zeros((halo, LANES), F32)
    padl_ref[0:halo, :] = zero_halo
    padl_ref[halo:halo + s_, :] = xl_ref[0]
    padl_ref[halo + s_:2 * halo + s_, :] = zero_halo
    padc_ref[0:halo, :] = zero_halo
    padc_ref[halo:halo + nc, :] = xc_ref[0]
    padc_ref[halo + nc:2 * halo + nc, :] = zero_halo

    cw = cw_ref[...]
    cb = cb_ref[...]

    def coeffs(pad_ref, j, d):
        win = pad_ref[pl.ds(pl.multiple_of(j * tc, SUBLANES), tc + 2 * halo), :]
        n = tc + 2 * halo
        u = (cb + cw[0:1] * pltpu.roll(win, 2, 0)[halo:halo + tc]
             + cw[1:2] * pltpu.roll(win, 1, 0)[halo:halo + tc]
             + cw[2:3] * win[halo:halo + tc]
             + cw[3:4] * pltpu.roll(win, n - 1, 0)[halo:halo + tc])
        gates = jnp.dot(u.astype(BF16), wg_ref[d, 0], preferred_element_type=F32) + bg_ref[d, 0]
        r = jax.nn.sigmoid(gates[:, :LANES])
        i = jax.nn.sigmoid(gates[:, LANES:])
        z = -lam_ref[d]
        softplus = jnp.maximum(z, 0.0) + jnp.log(1.0 + jnp.exp(-jnp.abs(z)))
        log_a = (-RG_C) * r * softplus
        a = jnp.exp(log_a)
        b = jnp.sqrt(1.0 - jnp.exp(2.0 * log_a)) * (i * u)
        return a, b

    def step(pad_ref, j, d, carry):
        a, b = coeffs(pad_ref, j, d)
        a_cum, b_loc = _scan_chunk(a, b, reverse=(d == 1))
        h = a_cum * carry + b_loc
        nxt = h[0:1] if d == 1 else h[tc - 1:tc]
        return h, nxt

    carry = jnp.zeros((1, LANES), F32)
    for j in range(nc // tc):
        _, carry = step(padc_ref, j, 0, carry)

    def fwd_body(j, carry):
        h, nxt = step(padl_ref, j, 0, carry)
        hf_ref[pl.ds(pl.multiple_of(j * tc, tc), tc), :] = h
        return nxt

    lax.fori_loop(0, s_ // tc, fwd_body, carry)

    carry = jnp.zeros((1, LANES), F32)
    for j in reversed(range(nc // tc)):
        _, carry = step(padc_ref, j, 1, carry)

    def bwd_body(jj, carry):
        j = s_ // tc - 1 - jj
        h, nxt = step(padl_ref, j, 1, carry)
        rows = pl.ds(pl.multiple_of(j * tc, tc), tc)
        o_ref[0, rows, :] = ((hf_ref[rows, :] + h) * gl_ref[0, rows, :].astype(F32)).astype(BF16)
        return nxt

    lax.fori_loop(0, s_ // tc, bwd_body, carry)


def _rglru(lx_lat, lx_ctx, gl, conv_w, conv_b, wg, bg, lam):
    b_, s_, w = lx_lat.shape
    nc = lx_ctx.shape[1]
    ng = w // LANES
    halo = SUBLANES
    return pl.pallas_call(
        _lru_kernel,
        out_shape=jax.ShapeDtypeStruct((b_, s_, w), BF16),
        grid=(b_, ng),
        in_specs=[pl.BlockSpec((1, s_, LANES), lambda b, g: (b, 0, g)),
                  pl.BlockSpec((1, nc, LANES), lambda b, g: (b, 0, g)),
                  pl.BlockSpec((1, s_, LANES), lambda b, g: (b, 0, g)),
                  pl.BlockSpec((CONV_W, LANES), lambda b, g: (0, g)),
                  pl.BlockSpec((1, LANES), lambda b, g: (0, g)),
                  pl.BlockSpec((2, 1, LANES, 2 * LANES), lambda b, g: (0, g, 0, 0)),
                  pl.BlockSpec((2, 1, 1, 2 * LANES), lambda b, g: (0, g, 0, 0)),
                  pl.BlockSpec((2, 1, LANES), lambda b, g: (0, 0, g))],
        out_specs=pl.BlockSpec((1, s_, LANES), lambda b, g: (b, 0, g)),
        scratch_shapes=[pltpu.VMEM((s_ + 2 * halo, LANES), F32),
                        pltpu.VMEM((nc + 2 * halo, LANES), F32),
                        pltpu.VMEM((s_, LANES), F32)],
        compiler_params=_params("parallel", "parallel"),
        name="rglru",
    )(lx_lat, lx_ctx, gl, conv_w, conv_b, wg, bg, lam)


def _first_index(mask, iota, limit, axis):
    return jnp.min(jnp.where(mask, iota, limit), axis=axis, keepdims=True)


def _outproj_kernel(o_ref, y2_ref, x_ref, g1_ref, sh_ref, sc_ref, g2_ref, gf_ref, wo_ref, wrt_ref, rb_ref,
                    wsgu_ref, wsd_ref, xs_ref, fp_ref, e_ref, w_ref):
    half = wo_ref.shape[0] // 2
    mix = (jnp.dot(o_ref[0], wo_ref[0:half, :], preferred_element_type=F32)
           + jnp.dot(y2_ref[0], wo_ref[half:, :], preferred_element_type=F32))
    x1 = x_ref[0] + g1_ref[0] * mix
    f = _rms(x1, gf_ref[...]) * (1.0 + sc_ref[0]) + sh_ref[0]
    f16 = f.astype(BF16)
    tm = f.shape[0]

    logits = lax.dot_general(wrt_ref[...], f, (((1,), (1,)), ((), ())),
                             precision=lax.Precision.HIGHEST, preferred_element_type=F32)
    scores = jax.nn.sigmoid(logits)
    sel = scores + rb_ref[...]
    per = N_EXPERTS // N_GROUPS
    g3 = sel.reshape(N_GROUPS, per, tm)
    mem = lax.broadcasted_iota(jnp.int32, g3.shape, 1)
    m1 = jnp.max(g3, axis=1, keepdims=True)
    first = _first_index(g3 == m1, mem, per, 1)
    m2 = jnp.max(jnp.where(mem == first, NEG_INF, g3), axis=1, keepdims=True)
    gscore = (m1 + m2).reshape(N_GROUPS, tm)
    giota = lax.broadcasted_iota(jnp.int32, gscore.shape, 0)
    gmask = jnp.zeros(gscore.shape, F32)
    cur = gscore
    for _ in range(TOPK_GROUPS):
        mx = jnp.max(cur, axis=0, keepdims=True)
        pick = giota == _first_index(cur == mx, giota, N_GROUPS, 0)
        gmask = jnp.where(pick, 1.0, gmask)
        cur = jnp.where(pick, NEG_INF, cur)
    allowed = jnp.broadcast_to(gmask.reshape(N_GROUPS, 1, tm), g3.shape) > 0.0
    cur = jnp.where(allowed, g3, NEG_INF).reshape(N_EXPERTS, tm)
    eiota = lax.broadcasted_iota(jnp.int32, cur.shape, 0)
    picked_w = []
    for k in range(TOP_K):
        mx = jnp.max(cur, axis=0, keepdims=True)
        idx = _first_index(cur == mx, eiota, N_EXPERTS, 0)
        pick = eiota == idx
        e_ref[k:k + 1, :] = idx
        picked_w.append(jnp.sum(jnp.where(pick, scores, 0.0), axis=0, keepdims=True))
        cur = jnp.where(pick, NEG_INF, cur)
    wsum = picked_w[0]
    for k in range(1, TOP_K):
        wsum = wsum + picked_w[k]
    for k in range(TOP_K):
        w_ref[k:k + 1, :] = ROUTED_SCALE * picked_w[k] / wsum

    a = jnp.dot(f16, wsgu_ref[...], preferred_element_type=F32)
    ff = a.shape[1] // 2
    act = _silu(a[:, :ff]) * a[:, ff:]
    shared = jnp.dot(act.astype(BF16), wsd_ref[...], preferred_element_type=F32)
    xs_ref[0] = x1 + g2_ref[0] * shared

    d2 = f.shape[1] // 2
    lo = lax.bitcast_convert_type(f16[:, :d2].astype(F32), jnp.int32)
    hi = lax.bitcast_convert_type(f16[:, d2:].astype(F32), jnp.int32)
    fp_ref[0] = lax.shift_right_logical(lo, 16) | (hi & jnp.int32(-65536))


def _output_projection(o, y2, x, g1, sh2, sc2, g2, gf, wts):
    b_, s_, d = x.shape
    tm = min(TOKEN_TILE, s_)
    nt = s_ // tm
    mod_spec = pl.BlockSpec((1, 1, d), lambda b, i: (b, 0, 0))

    def const(a):
        return pl.BlockSpec(a.shape, lambda b, i: (0,) * a.ndim)

    wide = lambda w: pl.BlockSpec((1, tm, w), lambda b, i: (b, i, 0))
    route_spec = pl.BlockSpec((TOP_K, tm), lambda b, i: (0, b * nt + i))
    weights = (wts["w_out"], wts["router_t"], wts["router_b"], wts["sh_gu"], wts["sh_d"])
    return pl.pallas_call(
        _outproj_kernel,
        out_shape=[jax.ShapeDtypeStruct((b_, s_, d), F32),
                   jax.ShapeDtypeStruct((b_, s_, d // 2), jnp.int32),
                   jax.ShapeDtypeStruct((TOP_K, b_ * s_), jnp.int32),
                   jax.ShapeDtypeStruct((TOP_K, b_ * s_), F32)],
        grid=(b_, nt),
        in_specs=[wide(o.shape[2]), wide(y2.shape[2]), wide(d), mod_spec, mod_spec, mod_spec, mod_spec,
                  const(gf)] + [const(w) for w in weights],
        out_specs=[wide(d), wide(d // 2), route_spec, route_spec],
        compiler_params=_params("parallel", "parallel"),
        name="output_projection",
    )(o, y2, x, g1, sh2, sc2, g2, gf, *weights)


def _moe_kernel(offs_ref, tok_ref, wl_ref, fp_ref, wgu_ref, wd_ref, acc_ref, tin_ref, tout_ref):
    e = pl.program_id(0)
    m = EXPERT_TILE
    st = TILE_STRIDE
    n_in = fp_ref.shape[0] // (acc_ref.shape[0] // SUBLANES)

    @pl.when(e == 0)
    def _():
        acc_ref[...] = jnp.zeros(acc_ref.shape, F32)

    start = offs_ref[e]
    count = offs_ref[e + 1] - start
    last = start + count - 1

    def rmw_load(i, row):
        tk = tok_ref[i]
        dst = pl.ds(pl.multiple_of(tk * SUBLANES, SUBLANES), SUBLANES)
        new = acc_ref[dst, :] + wl_ref[i] * tout_ref[pl.ds(row, SUBLANES, stride=st), :]
        return dst, new

    def tile_body(t, carry):
        base = start + t * m
        for mi in range(m):
            tk = tok_ref[jnp.minimum(base + mi, last)]
            slab = fp_ref[pl.ds(pl.multiple_of(tk * n_in, n_in), n_in), :]
            tin_ref[pl.ds(mi, n_in, stride=st), :] = slab
        words = [tin_ref[j * st:j * st + m, :] for j in range(n_in)]
        lo = [lax.bitcast_convert_type(lax.shift_left(w, 16), F32).astype(BF16) for w in words]
        hi = [lax.bitcast_convert_type(w & jnp.int32(-65536), F32).astype(BF16) for w in words]
        xt = jnp.concatenate(lo + hi, axis=-1)
        hcat = jnp.dot(xt, wgu_ref[0], preferred_element_type=F32)
        ff = hcat.shape[1] // 2
        act = _silu(hcat[:, :ff]) * hcat[:, ff:]
        y = jnp.dot(act.astype(BF16), wd_ref[0], preferred_element_type=F32)
        for j in range(y.shape[1] // LANES):
            tout_ref[j * st:j * st + m, :] = y[:, j * LANES:(j + 1) * LANES]

        nv = jnp.minimum(m, count - t * m)
        nfull = nv // SCATTER_UNROLL

        def group(g, c):
            r0 = g * SCATTER_UNROLL
            pend = [rmw_load(base + r0 + r, r0 + r) for r in range(SCATTER_UNROLL)]
            for dst, new in pend:
                acc_ref[dst, :] = new
            return c

        lax.fori_loop(0, nfull, group, 0)
        rem = nv - nfull * SCATTER_UNROLL
        for r in range(SCATTER_UNROLL - 1):
            @pl.when(r < rem)
            def _():
                row = nfull * SCATTER_UNROLL + r
                dst, new = rmw_load(base + row, row)
                acc_ref[dst, :] = new
        return carry

    lax.fori_loop(0, (count + m - 1) // m, tile_body, 0)


def _routed_experts(offs, tok, wl, fp4, wgu, wd, n_tok):
    d = wd.shape[2]
    n_e = wgu.shape[0]
    rows_in = fp4.shape[0]
    smem = pl.BlockSpec(memory_space=pltpu.SMEM)
    stage_in = (rows_in // n_tok) * TILE_STRIDE
    stage_in = (stage_in + SUBLANES - 1) // SUBLANES * SUBLANES
    return pl.pallas_call(
        _moe_kernel,
        out_shape=jax.ShapeDtypeStruct((n_tok * SUBLANES, LANES), F32),
        grid=(n_e,),
        in_specs=[smem, smem, smem,
                  pl.BlockSpec((rows_in, LANES), lambda e: (0, 0), pipeline_mode=pl.Buffered(1)),
                  pl.BlockSpec((1,) + wgu.shape[1:], lambda e: (e, 0, 0)),
                  pl.BlockSpec((1,) + wd.shape[1:], lambda e: (e, 0, 0))],
        out_specs=pl.BlockSpec((n_tok * SUBLANES, LANES), lambda e: (0, 0), pipeline_mode=pl.Buffered(1)),
        scratch_shapes=[pltpu.VMEM((stage_in, LANES), jnp.int32),
                        pltpu.VMEM((d // LANES * TILE_STRIDE, LANES), F32)],
        compiler_params=_params("arbitrary"),
        name="routed_experts",
    )(offs, tok, wl, fp4, wgu, wd)


def _final_kernel(xs_ref, r_ref, g2_ref, g_ref, o_ref):
    o_ref[0] = _rms(xs_ref[0] + g2_ref[0] * r_ref[0], g_ref[...])


def _final(xs, routed, g2, g):
    b_, s_, d = xs.shape
    tm = min(TOKEN_TILE, s_)
    wide = pl.BlockSpec((1, tm, d), lambda b, i: (b, i, 0))
    return pl.pallas_call(
        _final_kernel,
        out_shape=jax.ShapeDtypeStruct((b_, s_, d), F32),
        grid=(b_, s_ // tm),
        in_specs=[wide, wide, pl.BlockSpec((1, 1, d), lambda b, i: (b, 0, 0)),
                  pl.BlockSpec((1, d), lambda b, i: (0, 0))],
        out_specs=wide,
        compiler_params=_params("parallel", "parallel"),
        name="final_norm",
    )(xs, routed, g2, g)


def _prep_weights(w_in, q_norm_g, w_q_up, kv_norm_g, w_kv_up, lru_w_a, lru_b_a, lru_w_x, lru_b_x, w_out,
                  router_w, router_bias, exp_w_gate, exp_w_up, exp_w_down, sh_w_gate, sh_w_up, sh_w_down):
    d = w_in.shape[0]
    h_ = MLA_HEADS
    pad_kr = jnp.zeros((d, LANES - MLA_ROPE), F32)
    w_in_p = jnp.concatenate([w_in[:, :COL_LRU_X], pad_kr, w_in[:, COL_LRU_X:]], axis=1).astype(BF16)

    wq = w_q_up.reshape(MLA_Q_RANK, h_, MLA_NOPE + MLA_ROPE)
    nope, rope = wq[:, :, :MLA_NOPE], wq[:, :, MLA_NOPE:]
    rope_sw = rope.reshape(MLA_Q_RANK, h_, 2, 2, ROPE_AXIS // 2)[:, :, :, ::-1, :].reshape(rope.shape)
    zpad = jnp.zeros((MLA_Q_RANK, h_, LANES - MLA_NOPE - MLA_ROPE), F32)
    wq_p = jnp.concatenate([nope, rope, zpad], axis=-1).reshape(MLA_Q_RANK, h_ * LANES).astype(BF16)
    wq_sw = jnp.concatenate([jnp.zeros_like(nope), rope_sw, zpad], axis=-1).reshape(MLA_Q_RANK, h_ * LANES).astype(BF16)

    wkv = w_kv_up.reshape(MLA_KV_RANK, h_, MLA_NOPE + MLA_V)
    k_nope, v_w = wkv[:, :, :MLA_NOPE], wkv[:, :, MLA_NOPE:]
    r_idx = jnp.arange(MLA_ROPE)
    place = jnp.zeros((LANES, h_, LANES), F32).at[r_idx, :, MLA_NOPE + r_idx].set(1.0)
    place_sw = jnp.zeros((LANES, h_, LANES), F32).at[r_idx ^ (ROPE_AXIS // 2), :, MLA_NOPE + r_idx].set(1.0)
    k_top = jnp.concatenate([k_nope, jnp.zeros((MLA_KV_RANK, h_, LANES - MLA_NOPE), F32)], axis=-1)
    wk = jnp.concatenate([k_top, place], axis=0).reshape(MLA_KV_RANK + LANES, h_ * LANES).astype(BF16)
    wk_sw = jnp.concatenate([jnp.zeros_like(k_top), place_sw], axis=0).reshape(MLA_KV_RANK + LANES, h_ * LANES).astype(BF16)
    zv = jnp.zeros_like(v_w)
    even = (jnp.arange(h_) % 2 == 0)[None, :, None]
    wv = jnp.concatenate([jnp.where(even, v_w, zv), jnp.where(even, zv, v_w)], axis=-1)
    wv = wv.reshape(MLA_KV_RANK, h_ * LANES).astype(BF16)

    eye = jnp.eye(LRU_BLOCKS, dtype=F32)
    ng = LRU_WIDTH // LANES

    def dense(w):
        return jnp.einsum("xncd,nm->xncmd", w, eye).reshape(2, LRU_WIDTH, LRU_WIDTH)

    def grp(wd_):
        return jnp.stack([wd_[:, g * LANES:(g + 1) * LANES, g * LANES:(g + 1) * LANES] for g in range(ng)], axis=1)

    wg = jnp.concatenate([grp(dense(lru_w_a)), grp(dense(lru_w_x))], axis=-1).astype(BF16)
    bg = jnp.concatenate([lru_b_a.reshape(2, ng, 1, LANES), lru_b_x.reshape(2, ng, 1, LANES)], axis=-1)

    return dict(
        w_in=w_in_p, q_g=q_norm_g[None], kv_g=kv_norm_g[None], wq=wq_p, wq_sw=wq_sw, wk=wk, wk_sw=wk_sw, wv=wv,
        wg=wg, bg=bg,
        w_out=w_out.astype(BF16), router_t=router_w.T, router_b=router_bias[:, None],
        sh_gu=jnp.concatenate([sh_w_gate, sh_w_up], axis=1).astype(BF16), sh_d=sh_w_down.astype(BF16),
        exp_gu=jnp.concatenate([exp_w_gate, exp_w_up], axis=2).astype(BF16), exp_d=exp_w_down.astype(BF16),
    )


def _rope_tables(s_, nc):
    rows = s_ // GRID_W
    row = jnp.repeat(jnp.arange(rows, dtype=F32), GRID_W)
    col = jnp.tile(jnp.arange(GRID_W, dtype=F32), rows)
    inv_freq = ROPE_BASE ** (-jnp.arange(0, ROPE_AXIS, 2, dtype=F32) / ROPE_AXIS)
    ang = jnp.stack([row, col], axis=-1)[:, :, None] * inv_freq
    cos = jnp.broadcast_to(jnp.cos(ang)[:, :, None, :], (s_, 2, 2, ROPE_AXIS // 2)).reshape(s_, MLA_ROPE)
    sin = jnp.sin(ang)[:, :, None, :] * jnp.array([-1.0, 1.0], F32)[None, None, :, None]
    sin = sin.reshape(s_, MLA_ROPE)
    ones = jnp.ones((s_, MLA_NOPE), F32)
    zeros = jnp.zeros((s_, LANES - MLA_NOPE - MLA_ROPE), F32)
    c_tab = jnp.concatenate([ones, cos, zeros], axis=1)
    s_tab = jnp.concatenate([jnp.zeros_like(ones), sin, zeros], axis=1)
    c_ctx = jnp.concatenate([jnp.ones((nc, MLA_NOPE + MLA_ROPE), F32),
                             jnp.zeros((nc, LANES - MLA_NOPE - MLA_ROPE), F32)], axis=1)
    return c_tab, s_tab, c_ctx, jnp.zeros_like(c_ctx)


def _dispatch(eidx, wts, b_, s_):
    n = TOP_K * s_
    e = eidx.reshape(TOP_K, b_, s_).transpose(1, 0, 2).reshape(b_, n)
    w = wts.reshape(TOP_K, b_, s_).transpose(1, 0, 2).reshape(b_, n)
    key = e * n + jnp.arange(n, dtype=jnp.int32)[None, :]
    key_sorted, w_sorted = lax.sort((key, w), dimension=1, num_keys=1)
    tok = (key_sorted % n) % s_
    bounds = jnp.arange(N_EXPERTS + 1, dtype=jnp.int32) * n
    offs = jax.vmap(lambda k: jnp.searchsorted(k, bounds, side="left"))(key_sorted).astype(jnp.int32)
    return offs, tok.astype(jnp.int32), w_sorted


def kernel(x, c, ctx, c_ctx, w_mod, b_mod, norm_mix_g, w_in, q_norm_g, w_q_up, kv_norm_g, w_kv_up, conv_w, conv_b,
           lru_w_a, lru_b_a, lru_w_x, lru_b_x, lru_lambda, w_out, norm_ffn_g, router_w, router_bias, exp_w_gate,
           exp_w_up, exp_w_down, sh_w_gate, sh_w_up, sh_w_down, final_norm_g):
    b_, s_, d = x.shape
    nc = ctx.shape[1]
    assert w_mod.shape[0] == 1, "single-layer operation"
    assert s_ % GRID_W == 0 and s_ % SCAN_CHUNK == 0 and nc % SCAN_CHUNK == 0
    assert s_ % min(TOKEN_TILE, s_) == 0 and s_ % min(Q_TILE, s_) == 0

    wts = _prep_weights(w_in[0], q_norm_g[0], w_q_up[0], kv_norm_g[0], w_kv_up[0], lru_w_a[0], lru_b_a[0],
                        lru_w_x[0], lru_b_x[0], w_out[0], router_w[0], router_bias[0], exp_w_gate[0], exp_w_up[0],
                        exp_w_down[0], sh_w_gate[0], sh_w_up[0], sh_w_down[0])
    c_tab, s_tab, c_ctx_tab, s_ctx_tab = _rope_tables(s_, nc)

    rows = (b_ + 1 + SUBLANES - 1) // SUBLANES * SUBLANES
    cc = jnp.zeros((rows, d), F32).at[:b_].set(c).at[b_].set(c_ctx)
    mod = _modulation(cc, w_mod[0], b_mod[0][None])
    sh1, sc1, g1, sh2, sc2, g2 = [mod[:b_, i * d:(i + 1) * d].reshape(b_, 1, d) for i in range(6)]
    csh1, csc1 = [mod[b_, i * d:(i + 1) * d].reshape(1, 1, d) for i in range(2)]

    q, k_lat, v_lat, lx_lat, gl = _input_projection(
        x, sh1, sc1, norm_mix_g, wts, (c_tab * ATTN_SCALE, s_tab * ATTN_SCALE, c_tab, s_tab), True)
    k_ctx, v_ctx, lx_ctx, _ = _input_projection(
        ctx, csh1, csc1, norm_mix_g, wts, (c_ctx_tab, s_ctx_tab, c_ctx_tab, s_ctx_tab), False)

    o_lat = _attention(q, k_lat, k_ctx, v_lat, v_ctx)
    y2 = _rglru(lx_lat, lx_ctx, gl, conv_w[0], conv_b[0][None], wts["wg"], wts["bg"],
                lru_lambda[0].reshape(2, 1, LRU_WIDTH))

    xs, fp, eidx, rw = _output_projection(o_lat, y2, x, g1, sh2, sc2, g2, norm_ffn_g, wts)

    offs, tok, wl = _dispatch(eidx, rw, b_, s_)
    fp4 = fp.reshape(b_, s_ * (d // 2 // LANES), LANES)
    routed = [_routed_experts(offs[b], tok[b], wl[b], fp4[b], wts["exp_gu"], wts["exp_d"], s_) for b in range(b_)]
    routed = jnp.stack(routed).reshape(b_, s_, d)

    return _final(xs, routed, g2, final_norm_g[None])
```

```python
import functools

import jax
import jax.numpy as jnp
from jax import lax
from jax.experimental import pallas as pl
from jax.experimental.pallas import tpu as pltpu

GRID_W = 64
EPS = 1e-6
MLA_HEADS = 8
MLA_NOPE = 64
MLA_ROPE = 32
MLA_V = 64
MLA_Q_RANK = 256
MLA_KV_RANK = 128
LRU_WIDTH = 512
LRU_BLOCKS = 8
CONV_W = 4
RG_C = 8.0
ROPE_BASE = 10000.0
ROPE_AXIS = MLA_ROPE // 2
ATTN_SCALE = (MLA_NOPE + MLA_ROPE) ** -0.5
N_EXPERTS = 64
TOP_K = 8
N_GROUPS = 8
TOPK_GROUPS = 4
EXPERT_FF = 256
ROUTED_SCALE = 2.5
COL_KV = MLA_Q_RANK
COL_KR = COL_KV + MLA_KV_RANK
COL_LRU_X = COL_KR + MLA_ROPE
COL_LRU_G = COL_LRU_X + LRU_WIDTH

LANES = 128
SUBLANES = 8
VMEM_LIMIT = 56 * 1024 * 1024

TOKEN_TILE = 512
Q_TILE = 256
SCAN_CHUNK = 128
EXPERT_TILE = 256
TILE_STRIDE = EXPERT_TILE + 1
SCATTER_UNROLL = 8

F32 = jnp.float32
BF16 = jnp.bfloat16
NEG_INF = float("-inf")


def _params(*sem):
    return pltpu.CompilerParams(dimension_semantics=sem, vmem_limit_bytes=VMEM_LIMIT)


def _rms(t, g):
    return t * lax.rsqrt(jnp.mean(t * t, axis=-1, keepdims=True) + EPS) * g


def _gelu_tanh(t):
    return 0.5 * t * (1.0 + jnp.tanh(0.7978845608028654 * (t + 0.044715 * (t * t * t))))


def _silu(t):
    return t * jax.nn.sigmoid(t)


def _nt_dot(a, b):
    return lax.dot_general(a, b, (((1,), (1,)), ((), ())), preferred_element_type=F32)


def _mod_kernel(c_ref, w_ref, b_ref, o_ref):
    s = _silu(c_ref[...])
    o_ref[...] = jnp.dot(s, w_ref[...], precision=lax.Precision.HIGHEST,
                         preferred_element_type=F32) + b_ref[...]


def _modulation(cc, w, b):
    rows, d = cc.shape
    n = w.shape[1]
    tn = 1024
    return pl.pallas_call(
        _mod_kernel,
        out_shape=jax.ShapeDtypeStruct((rows, n), F32),
        grid=(n // tn,),
        in_specs=[pl.BlockSpec((rows, d), lambda j: (0, 0)),
                  pl.BlockSpec((d, tn), lambda j: (0, j)),
                  pl.BlockSpec((1, tn), lambda j: (0, j))],
        out_specs=pl.BlockSpec((rows, tn), lambda j: (0, j)),
        compiler_params=_params("arbitrary"),
        name="modulation",
    )(cc, w, b)


def _inproj_kernel(*refs, with_q):
    (x_ref, sh_ref, sc_ref, g_ref, win_ref, qg_ref, kvg_ref, wq_ref, wqs_ref, wk_ref, wks_ref,
     wv_ref, cq_ref, sq_ref, ck_ref, sk_ref) = refs[:16]
    if with_q:
        q_ref, k_ref, v_ref, lx_ref, lg_ref = refs[16:]
    else:
        k_ref, v_ref, lx_ref, lg_ref = refs[16:]
    h = _rms(x_ref[0], g_ref[...]) * (1.0 + sc_ref[0]) + sh_ref[0]
    p = jnp.dot(h.astype(BF16), win_ref[...], preferred_element_type=F32)

    if with_q:
        qn = _rms(p[:, :MLA_Q_RANK], qg_ref[...]).astype(BF16)
        qa = jnp.dot(qn, wq_ref[...], preferred_element_type=F32)
        qb = jnp.dot(qn, wqs_ref[...], preferred_element_type=F32)
        cq, sq = cq_ref[...], sq_ref[...]
        for hd in range(MLA_HEADS):
            sl = slice(hd * LANES, (hd + 1) * LANES)
            q_ref[0, :, sl] = (qa[:, sl] * cq + qb[:, sl] * sq).astype(BF16)

    kvn = _rms(p[:, COL_KV:COL_KR], kvg_ref[...])
    kvn16 = kvn.astype(BF16)
    kin = jnp.concatenate([kvn16, p[:, COL_KR:COL_KR + LANES].astype(BF16)], axis=-1)
    ka = jnp.dot(kin, wk_ref[...], preferred_element_type=F32)
    kb = jnp.dot(kin, wks_ref[...], preferred_element_type=F32)
    ck, sk = ck_ref[...], sk_ref[...]
    for hd in range(MLA_HEADS):
        sl = slice(hd * LANES, (hd + 1) * LANES)
        k_ref[0, :, sl] = (ka[:, sl] * ck + kb[:, sl] * sk).astype(BF16)
    v_ref[0] = jnp.dot(kvn16, wv_ref[...], preferred_element_type=F32).astype(BF16)
    lx_ref[0] = p[:, 512:512 + LRU_WIDTH]
    lg_ref[0] = _gelu_tanh(p[:, 1024:1024 + LRU_WIDTH]).astype(BF16)


def _input_projection(x, sh, sc, g, wts, tabs, with_q):
    b_, s_, d = x.shape
    tm = min(TOKEN_TILE, s_)
    hw = MLA_HEADS * LANES
    per_batch = sh.shape[0] == b_
    mod_spec = pl.BlockSpec((1, 1, d), (lambda b, i: (b, 0, 0)) if per_batch else (lambda b, i: (0, 0, 0)))

    def const(a):
        return pl.BlockSpec(a.shape, lambda b, i: (0,) * a.ndim)

    tab_spec = pl.BlockSpec((tm, LANES), lambda b, i: (i, 0))
    wide = lambda w: pl.BlockSpec((1, tm, w), lambda b, i: (b, i, 0))
    weights = (wts["w_in"], wts["q_g"], wts["kv_g"], wts["wq"], wts["wq_sw"], wts["wk"], wts["wk_sw"], wts["wv"])
    out_shape = [jax.ShapeDtypeStruct((b_, s_, hw), BF16),
                 jax.ShapeDtypeStruct((b_, s_, hw), BF16),
                 jax.ShapeDtypeStruct((b_, s_, LRU_WIDTH), F32),
                 jax.ShapeDtypeStruct((b_, s_, LRU_WIDTH), BF16)]
    out_specs = [wide(hw), wide(hw), wide(LRU_WIDTH), wide(LRU_WIDTH)]
    if with_q:
        out_shape = [jax.ShapeDtypeStruct((b_, s_, hw), BF16)] + out_shape
        out_specs = [wide(hw)] + out_specs
    return pl.pallas_call(
        functools.partial(_inproj_kernel, with_q=with_q),
        out_shape=out_shape,
        grid=(b_, s_ // tm),
        in_specs=[wide(d), mod_spec, mod_spec, const(g)] + [const(w) for w in weights] + [tab_spec] * 4,
        out_specs=out_specs,
        compiler_params=_params("parallel", "parallel"),
        name="input_projection_lat" if with_q else "input_projection_ctx",
    )(x, sh, sc, g, *weights, *tabs)


def _attn_kernel(q_ref, kl_ref, kc_ref, vl_ref, vc_ref, o_ref):
    acc = None
    for a in range(2):
        sl = slice(a * LANES, (a + 1) * LANES)
        q = q_ref[0, :, sl]
        s1 = _nt_dot(q, kl_ref[0, :, sl])
        s2 = _nt_dot(q, kc_ref[0, :, sl])
        m = jnp.maximum(jnp.max(s1, axis=-1, keepdims=True), jnp.max(s2, axis=-1, keepdims=True))
        p1 = jnp.exp(s1 - m)
        p2 = jnp.exp(s2 - m)
        l = jnp.sum(p1, axis=-1, keepdims=True) + jnp.sum(p2, axis=-1, keepdims=True)
        o = (jnp.dot(p1.astype(BF16), vl_ref[0, :, sl], preferred_element_type=F32)
             + jnp.dot(p2.astype(BF16), vc_ref[0, :, sl], preferred_element_type=F32))
        o = o * (1.0 / l)
        acc = o if acc is None else acc + o
    o_ref[0] = acc.astype(BF16)


def _attention(q, k_lat, k_ctx, v_lat, v_ctx):
    b_, s_, hw = q.shape
    nc = k_ctx.shape[1]
    tq = min(Q_TILE, s_)
    pair = 2 * LANES
    n_pairs = hw // pair
    return pl.pallas_call(
        _attn_kernel,
        out_shape=jax.ShapeDtypeStruct((b_, s_, n_pairs * LANES), BF16),
        grid=(b_, n_pairs, s_ // tq),
        in_specs=[pl.BlockSpec((1, tq, pair), lambda b, j, i: (b, i, j)),
                  pl.BlockSpec((1, s_, pair), lambda b, j, i: (b, 0, j)),
                  pl.BlockSpec((1, nc, pair), lambda b, j, i: (b, 0, j)),
                  pl.BlockSpec((1, s_, pair), lambda b, j, i: (b, 0, j)),
                  pl.BlockSpec((1, nc, pair), lambda b, j, i: (b, 0, j))],
        out_specs=pl.BlockSpec((1, tq, LANES), lambda b, j, i: (b, i, j)),
        compiler_params=_params("parallel", "parallel", "arbitrary"),
        name="attention",
    )(q, k_lat, k_ctx, v_lat, v_ctx)


def _scan_chunk(a, b, reverse):
    tc = a.shape[0]
    row = lax.broadcasted_iota(jnp.int32, a.shape, 0)
    s = 1
    while s < tc:
        if s < SUBLANES:
            if reverse:
                keep = row < tc - s
                a_sh = jnp.where(keep, pltpu.roll(a, tc - s, 0), 1.0)
                b_sh = jnp.where(keep, pltpu.roll(b, tc - s, 0), 0.0)
            else:
                keep = row >= s
                a_sh = jnp.where(keep, pltpu.roll(a, s, 0), 1.0)
                b_sh = jnp.where(keep, pltpu.roll(b, s, 0), 0.0)
        else:
            ones = jnp.ones((s, a.shape[1]), F32)
            zeros = jnp.zeros((s, a.shape[1]), F32)
            if reverse:
                a_sh = jnp.concatenate([a[s:], ones], axis=0)
                b_sh = jnp.concatenate([b[s:], zeros], axis=0)
            else:
                a_sh = jnp.concatenate([ones, a[:tc - s]], axis=0)
                b_sh = jnp.concatenate([zeros, b[:tc - s]], axis=0)
        b = a * b_sh + b
        a = a * a_sh
        s *= 2
    return a, b


def _lru_kernel(xl_ref, xc_ref, gl_ref, cw_ref, cb_ref, wg_ref, bg_ref, lam_ref, o_ref,
                padl_ref, padc_ref, hf_ref):
    s_ = xl_ref.shape[1]
    nc = xc_ref.shape[1]
    tc = SCAN_CHUNK
    halo = SUBLANES
    zero_halo = jnp.zeros((halo, LANES), F32)
    padl_ref[0:halo, :] = zero_halo
    padl_ref[halo:halo + s_, :] = xl_ref[0]
    padl_ref[halo + s_:2 * halo + s_, :] = zero_halo
    padc_ref[0:halo, :] = zero_halo
    padc_ref[halo:halo + nc, :] = xc_ref[0]
    padc_ref[halo + nc:2 * halo + nc, :] = zero_halo

    cw = cw_ref[...]
    cb = cb_ref[...]

    def coeffs(pad_ref, j, d):
        win = pad_ref[pl.ds(pl.multiple_of(j * tc, SUBLANES), tc + 2 * halo), :]
        n = tc + 2 * halo
        u = (cb + cw[0:1] * pltpu.roll(win, 2, 0)[halo:halo + tc]
             + cw[1:2] * pltpu.roll(win, 1, 0)[halo:halo + tc]
             + cw[2:3] * win[halo:halo + tc]
             + cw[3:4] * pltpu.roll(win, n - 1, 0)[halo:halo + tc])
        gates = jnp.dot(u.astype(BF16), wg_ref[d, 0], preferred_element_type=F32) + bg_ref[d, 0]
        r = jax.nn.sigmoid(gates[:, :LANES])
        i = jax.nn.sigmoid(gates[:, LANES:])
        z = -lam_ref[d]
        softplus = jnp.maximum(z, 0.0) + jnp.log(1.0 + jnp.exp(-jnp.abs(z)))
        log_a = (-RG_C) * r * softplus
        a = jnp.exp(log_a)
        b = jnp.sqrt(1.0 - jnp.exp(2.0 * log_a)) * (i * u)
        return a, b

    def step(pad_ref, j, d, carry):
        a, b = coeffs(pad_ref, j, d)
        a_cum, b_loc = _scan_chunk(a, b, reverse=(d == 1))
        h = a_cum * carry + b_loc
        nxt = h[0:1] if d == 1 else h[tc - 1:tc]
        return h, nxt

    carry = jnp.zeros((1, LANES), F32)
    for j in range(nc // tc):
        _, carry = step(padc_ref, j, 0, carry)

    def fwd_body(j, carry):
        h, nxt = step(padl_ref, j, 0, carry)
        hf_ref[pl.ds(pl.multiple_of(j * tc, tc), tc), :] = h
        return nxt

    lax.fori_loop(0, s_ // tc, fwd_body, carry)

    carry = jnp.zeros((1, LANES), F32)
    for j in reversed(range(nc // tc)):
        _, carry = step(padc_ref, j, 1, carry)

    def bwd_body(jj, carry):
        j = s_ // tc - 1 - jj
        h, nxt = step(padl_ref, j, 1, carry)
        rows = pl.ds(pl.multiple_of(j * tc, tc), tc)
        o_ref[0, rows, :] = ((hf_ref[rows, :] + h) * gl_ref[0, rows, :].astype(F32)).astype(BF16)
        return nxt

    lax.fori_loop(0, s_ // tc, bwd_body, carry)


def _rglru(lx_lat, lx_ctx, gl, conv_w, conv_b, wg, bg, lam):
    b_, s_, w = lx_lat.shape
    nc = lx_ctx.shape[1]
    ng = w // LANES
    halo = SUBLANES
    return pl.pallas_call(
        _lru_kernel,
        out_shape=jax.ShapeDtypeStruct((b_, s_, w), BF16),
        grid=(b_, ng),
        in_specs=[pl.BlockSpec((1, s_, LANES), lambda b, g: (b, 0, g)),
                  pl.BlockSpec((1, nc, LANES), lambda b, g: (b, 0, g)),
                  pl.BlockSpec((1, s_, LANES), lambda b, g: (b, 0, g)),
                  pl.BlockSpec((CONV_W, LANES), lambda b, g: (0, g)),
                  pl.BlockSpec((1, LANES), lambda b, g: (0, g)),
                  pl.BlockSpec((2, 1, LANES, 2 * LANES), lambda b, g: (0, g, 0, 0)),
                  pl.BlockSpec((2, 1, 1, 2 * LANES), lambda b, g: (0, g, 0, 0)),
                  pl.BlockSpec((2, 1, LANES), lambda b, g: (0, 0, g))],
        out_specs=pl.BlockSpec((1, s_, LANES), lambda b, g: (b, 0, g)),
        scratch_shapes=[pltpu.VMEM((s_ + 2 * halo, LANES), F32),
                        pltpu.VMEM((nc + 2 * halo, LANES), F32),
                        pltpu.VMEM((s_, LANES), F32)],
        compiler_params=_params("parallel", "parallel"),
        name="rglru",
    )(lx_lat, lx_ctx, gl, conv_w, conv_b, wg, bg, lam)


def _first_index(mask, iota, limit, axis):
    return jnp.min(jnp.where(mask, iota, limit), axis=axis, keepdims=True)


def _outproj_kernel(o_ref, y2_ref, x_ref, g1_ref, sh_ref, sc_ref, g2_ref, gf_ref, wo_ref, wrt_ref, rb_ref,
                    wsgu_ref, wsd_ref, xs_ref, fp_ref, e_ref, w_ref):
    half = wo_ref.shape[0] // 2
    mix = (jnp.dot(o_ref[0], wo_ref[0:half, :], preferred_element_type=F32)
           + jnp.dot(y2_ref[0], wo_ref[half:, :], preferred_element_type=F32))
    x1 = x_ref[0] + g1_ref[0] * mix
    f = _rms(x1, gf_ref[...]) * (1.0 + sc_ref[0]) + sh_ref[0]
    f16 = f.astype(BF16)
    tm = f.shape[0]

    logits = lax.dot_general(wrt_ref[...], f, (((1,), (1,)), ((), ())),
                             precision=lax.Precision.HIGHEST, preferred_element_type=F32)
    scores = jax.nn.sigmoid(logits)
    sel = scores + rb_ref[...]
    per = N_EXPERTS // N_GROUPS
    g3 = sel.reshape(N_GROUPS, per, tm)
    mem = lax.broadcasted_iota(jnp.int32, g3.shape, 1)
    m1 = jnp.max(g3, axis=1, keepdims=True)
    first = _first_index(g3 == m1, mem, per, 1)
    m2 = jnp.max(jnp.where(mem == first, NEG_INF, g3), axis=1, keepdims=True)
    gscore = (m1 + m2).reshape(N_GROUPS, tm)
    giota = lax.broadcasted_iota(jnp.int32, gscore.shape, 0)
    gmask = jnp.zeros(gscore.shape, F32)
    cur = gscore
    for _ in range(TOPK_GROUPS):
        mx = jnp.max(cur, axis=0, keepdims=True)
        pick = giota == _first_index(cur == mx, giota, N_GROUPS, 0)
        gmask = jnp.where(pick, 1.0, gmask)
        cur = jnp.where(pick, NEG_INF, cur)
    allowed = jnp.broadcast_to(gmask.reshape(N_GROUPS, 1, tm), g3.shape) > 0.0
    cur = jnp.where(allowed, g3, NEG_INF).reshape(N_EXPERTS, tm)
    eiota = lax.broadcasted_iota(jnp.int32, cur.shape, 0)
    picked_w = []
    for k in range(TOP_K):
        mx = jnp.max(cur, axis=0, keepdims=True)
        idx = _first_index(cur == mx, eiota, N_EXPERTS, 0)
        pick = eiota == idx
        e_ref[k:k + 1, :] = idx
        picked_w.append(jnp.sum(jnp.where(pick, scores, 0.0), axis=0, keepdims=True))
        cur = jnp.where(pick, NEG_INF, cur)
    wsum = picked_w[0]
    for k in range(1, TOP_K):
        wsum = wsum + picked_w[k]
    for k in range(TOP_K):
        w_ref[k:k + 1, :] = ROUTED_SCALE * picked_w[k] / wsum

    a = jnp.dot(f16, wsgu_ref[...], preferred_element_type=F32)
    ff = a.shape[1] // 2
    act = _silu(a[:, :ff]) * a[:, ff:]
    shared = jnp.dot(act.astype(BF16), wsd_ref[...], preferred_element_type=F32)
    xs_ref[0] = x1 + g2_ref[0] * shared

    fp_ref[0] = f


def _output_projection(o, y2, x, g1, sh2, sc2, g2, gf, wts):
    b_, s_, d = x.shape
    tm = min(TOKEN_TILE, s_)
    nt = s_ // tm
    mod_spec = pl.BlockSpec((1, 1, d), lambda b, i: (b, 0, 0))

    def const(a):
        return pl.BlockSpec(a.shape, lambda b, i: (0,) * a.ndim)

    wide = lambda w: pl.BlockSpec((1, tm, w), lambda b, i: (b, i, 0))
    route_spec = pl.BlockSpec((TOP_K, tm), lambda b, i: (0, b * nt + i))
    weights = (wts["w_out"], wts["router_t"], wts["router_b"], wts["sh_gu"], wts["sh_d"])
    return pl.pallas_call(
        _outproj_kernel,
        out_shape=[jax.ShapeDtypeStruct((b_, s_, d), F32),
                   jax.ShapeDtypeStruct((b_, s_, d), F32),
                   jax.ShapeDtypeStruct((TOP_K, b_ * s_), jnp.int32),
                   jax.ShapeDtypeStruct((TOP_K, b_ * s_), F32)],
        grid=(b_, nt),
        in_specs=[wide(o.shape[2]), wide(y2.shape[2]), wide(d), mod_spec, mod_spec, mod_spec, mod_spec,
                  const(gf)] + [const(w) for w in weights],
        out_specs=[wide(d), wide(d), route_spec, route_spec],
        compiler_params=_params("parallel", "parallel"),
        name="output_projection",
    )(o, y2, x, g1, sh2, sc2, g2, gf, *weights)


def _moe_kernel(tile_e_ref, tile_src_ref, tile_n_ref, ntiles_ref, tok_ref, wl_ref,
                fp_ref, wgu_ref, wd_ref, acc_ref, tin_a, tin_b, tout_a, tout_b):
    t = pl.program_id(0)
    m = EXPERT_TILE
    st = TILE_STRIDE
    n_out = tout_a.shape[0] // st

    def slab(off):
        return pl.ds(pl.multiple_of(off, SUBLANES), SUBLANES)

    def gather(tile, tin):
        base = tile_src_ref[tile]
        for mi in range(m):
            tin[pl.ds(mi, SUBLANES, stride=st), :] = fp_ref[slab(tok_ref[base + mi]), :]

    def experts(tile, tin, tout):
        xt = jnp.concatenate([tin[j * st:j * st + m, :].astype(BF16) for j in range(n_out)], axis=-1)
        hcat = jnp.dot(xt, wgu_ref[0], preferred_element_type=F32)
        ff = hcat.shape[1] // 2
        act = _silu(hcat[:, :ff]) * hcat[:, ff:]
        y = jnp.dot(act.astype(BF16), wd_ref[0], preferred_element_type=F32)
        valid = lax.broadcasted_iota(jnp.int32, (m, LANES), 0) < tile_n_ref[tile]
        for j in range(n_out):
            tout[j * st:j * st + m, :] = jnp.where(valid, y[:, j * LANES:(j + 1) * LANES], 0.0)

    def scatter(tile, tout):
        base = tile_src_ref[tile]
        for g in range(m // SCATTER_UNROLL):
            pend = []
            for r in range(SCATTER_UNROLL):
                row = g * SCATTER_UNROLL + r
                dst = slab(tok_ref[base + row])
                pend.append((dst, acc_ref[dst, :] + wl_ref[base + row] * tout[pl.ds(row, SUBLANES, stride=st), :]))
            for dst, new in reversed(pend):
                acc_ref[dst, :] = new

    @pl.when(t == 0)
    def _():
        acc_ref[...] = jnp.zeros(acc_ref.shape, F32)
        tout_b[...] = jnp.zeros(tout_b.shape, F32)
        gather(0, tin_a)

    def step(tin_cur, tin_nxt, tout_cur, tout_prev):
        gather(t + 1, tin_nxt)
        experts(t, tin_cur, tout_cur)
        scatter(jnp.maximum(t - 1, 0), tout_prev)

    active = t <= ntiles_ref[0]

    @pl.when(active & (t % 2 == 0))
    def _():
        step(tin_a, tin_b, tout_a, tout_b)

    @pl.when(active & (t % 2 == 1))
    def _():
        step(tin_b, tin_a, tout_b, tout_a)


def _routed_experts(tables, tok, wl, fp4, wgu, wd, n_tok):
    tile_e, tile_src, tile_n, ntiles = tables
    d = wd.shape[2]
    rows_in = fp4.shape[0]
    assert rows_in == n_tok * SUBLANES and d == SUBLANES * LANES
    stage_out = stage_in = d // LANES * TILE_STRIDE
    n_steps = tile_e.shape[0] - 1
    grid_spec = pltpu.PrefetchScalarGridSpec(
        num_scalar_prefetch=6,
        grid=(n_steps,),
        in_specs=[pl.BlockSpec((rows_in, LANES), lambda t, *_: (0, 0), pipeline_mode=pl.Buffered(1)),
                  pl.BlockSpec((1,) + wgu.shape[1:], lambda t, te, *_: (te[t], 0, 0)),
                  pl.BlockSpec((1,) + wd.shape[1:], lambda t, te, *_: (te[t], 0, 0))],
        out_specs=pl.BlockSpec((n_tok * SUBLANES, LANES), lambda t, *_: (0, 0), pipeline_mode=pl.Buffered(1)),
        scratch_shapes=[pltpu.VMEM((stage_in, LANES), F32), pltpu.VMEM((stage_in, LANES), F32),
                        pltpu.VMEM((stage_out, LANES), F32), pltpu.VMEM((stage_out, LANES), F32)],
    )
    return pl.pallas_call(
        _moe_kernel,
        out_shape=jax.ShapeDtypeStruct((n_tok * SUBLANES, LANES), F32),
        grid_spec=grid_spec,
        compiler_params=_params("arbitrary"),
        name="routed_experts",
    )(tile_e, tile_src, tile_n, ntiles, tok, wl, fp4, wgu, wd)


def _final_kernel(xs_ref, r_ref, g2_ref, g_ref, o_ref):
    o_ref[0] = _rms(xs_ref[0] + g2_ref[0] * r_ref[0], g_ref[...])


def _final(xs, routed, g2, g):
    b_, s_, d = xs.shape
    tm = min(TOKEN_TILE, s_)
    wide = pl.BlockSpec((1, tm, d), lambda b, i: (b, i, 0))
    return pl.pallas_call(
        _final_kernel,
        out_shape=jax.ShapeDtypeStruct((b_, s_, d), F32),
        grid=(b_, s_ // tm),
        in_specs=[wide, wide, pl.BlockSpec((1, 1, d), lambda b, i: (b, 0, 0)),
                  pl.BlockSpec((1, d), lambda b, i: (0, 0))],
        out_specs=wide,
        compiler_params=_params("parallel", "parallel"),
        name="final_norm",
    )(xs, routed, g2, g)


def _prep_weights(w_in, q_norm_g, w_q_up, kv_norm_g, w_kv_up, lru_w_a, lru_b_a, lru_w_x, lru_b_x, w_out,
                  router_w, router_bias, exp_w_gate, exp_w_up, exp_w_down, sh_w_gate, sh_w_up, sh_w_down):
    d = w_in.shape[0]
    h_ = MLA_HEADS
    pad_kr = jnp.zeros((d, LANES - MLA_ROPE), F32)
    w_in_p = jnp.concatenate([w_in[:, :COL_LRU_X], pad_kr, w_in[:, COL_LRU_X:]], axis=1).astype(BF16)

    wq = w_q_up.reshape(MLA_Q_RANK, h_, MLA_NOPE + MLA_ROPE)
    nope, rope = wq[:, :, :MLA_NOPE], wq[:, :, MLA_NOPE:]
    rope_sw = rope.reshape(MLA_Q_RANK, h_, 2, 2, ROPE_AXIS // 2)[:, :, :, ::-1, :].reshape(rope.shape)
    zpad = jnp.zeros((MLA_Q_RANK, h_, LANES - MLA_NOPE - MLA_ROPE), F32)
    wq_p = jnp.concatenate([nope, rope, zpad], axis=-1).reshape(MLA_Q_RANK, h_ * LANES).astype(BF16)
    wq_sw = jnp.concatenate([jnp.zeros_like(nope), rope_sw, zpad], axis=-1).reshape(MLA_Q_RANK, h_ * LANES).astype(BF16)

    wkv = w_kv_up.reshape(MLA_KV_RANK, h_, MLA_NOPE + MLA_V)
    k_nope, v_w = wkv[:, :, :MLA_NOPE], wkv[:, :, MLA_NOPE:]
    r_idx = jnp.arange(MLA_ROPE)
    place = jnp.zeros((LANES, h_, LANES), F32).at[r_idx, :, MLA_NOPE + r_idx].set(1.0)
    place_sw = jnp.zeros((LANES, h_, LANES), F32).at[r_idx ^ (ROPE_AXIS // 2), :, MLA_NOPE + r_idx].set(1.0)
    k_top = jnp.concatenate([k_nope, jnp.zeros((MLA_KV_RANK, h_, LANES - MLA_NOPE), F32)], axis=-1)
    wk = jnp.concatenate([k_top, place], axis=0).reshape(MLA_KV_RANK + LANES, h_ * LANES).astype(BF16)
    wk_sw = jnp.concatenate([jnp.zeros_like(k_top), place_sw], axis=0).reshape(MLA_KV_RANK + LANES, h_ * LANES).astype(BF16)
    zv = jnp.zeros_like(v_w)
    even = (jnp.arange(h_) % 2 == 0)[None, :, None]
    wv = jnp.concatenate([jnp.where(even, v_w, zv), jnp.where(even, zv, v_w)], axis=-1)
    wv = wv.reshape(MLA_KV_RANK, h_ * LANES).astype(BF16)

    eye = jnp.eye(LRU_BLOCKS, dtype=F32)
    ng = LRU_WIDTH // LANES

    def dense(w):
        return jnp.einsum("xncd,nm->xncmd", w, eye).reshape(2, LRU_WIDTH, LRU_WIDTH)

    def grp(wd_):
        return jnp.stack([wd_[:, g * LANES:(g + 1) * LANES, g * LANES:(g + 1) * LANES] for g in range(ng)], axis=1)

    wg = jnp.concatenate([grp(dense(lru_w_a)), grp(dense(lru_w_x))], axis=-1).astype(BF16)
    bg = jnp.concatenate([lru_b_a.reshape(2, ng, 1, LANES), lru_b_x.reshape(2, ng, 1, LANES)], axis=-1)

    return dict(
        w_in=w_in_p, q_g=q_norm_g[None], kv_g=kv_norm_g[None], wq=wq_p, wq_sw=wq_sw, wk=wk, wk_sw=wk_sw, wv=wv,
        wg=wg, bg=bg,
        w_out=w_out.astype(BF16), router_t=router_w.T, router_b=router_bias[:, None],
        sh_gu=jnp.concatenate([sh_w_gate, sh_w_up], axis=1).astype(BF16), sh_d=sh_w_down.astype(BF16),
        exp_gu=jnp.concatenate([exp_w_gate, exp_w_up], axis=2).astype(BF16), exp_d=exp_w_down.astype(BF16),
    )


def _rope_tables(s_, nc):
    rows = s_ // GRID_W
    row = jnp.repeat(jnp.arange(rows, dtype=F32), GRID_W)
    col = jnp.tile(jnp.arange(GRID_W, dtype=F32), rows)
    inv_freq = ROPE_BASE ** (-jnp.arange(0, ROPE_AXIS, 2, dtype=F32) / ROPE_AXIS)
    ang = jnp.stack([row, col], axis=-1)[:, :, None] * inv_freq
    cos = jnp.broadcast_to(jnp.cos(ang)[:, :, None, :], (s_, 2, 2, ROPE_AXIS // 2)).reshape(s_, MLA_ROPE)
    sin = jnp.sin(ang)[:, :, None, :] * jnp.array([-1.0, 1.0], F32)[None, None, :, None]
    sin = sin.reshape(s_, MLA_ROPE)
    ones = jnp.ones((s_, MLA_NOPE), F32)
    zeros = jnp.zeros((s_, LANES - MLA_NOPE - MLA_ROPE), F32)
    c_tab = jnp.concatenate([ones, cos, zeros], axis=1)
    s_tab = jnp.concatenate([jnp.zeros_like(ones), sin, zeros], axis=1)
    c_ctx = jnp.concatenate([jnp.ones((nc, MLA_NOPE + MLA_ROPE), F32),
                             jnp.zeros((nc, LANES - MLA_NOPE - MLA_ROPE), F32)], axis=1)
    return c_tab, s_tab, c_ctx, jnp.zeros_like(c_ctx)


def _dispatch(eidx, wts, b_, s_):
    n = TOP_K * s_
    e = eidx.reshape(TOP_K, b_, s_).transpose(1, 0, 2).reshape(b_, n)
    w = wts.reshape(TOP_K, b_, s_).transpose(1, 0, 2).reshape(b_, n)
    key = e * n + jnp.arange(n, dtype=jnp.int32)[None, :]
    key_sorted, w_sorted = lax.sort((key, w), dimension=1, num_keys=1)
    tok = ((key_sorted % n) % s_) * SUBLANES
    tok = jnp.pad(tok.astype(jnp.int32), ((0, 0), (0, EXPERT_TILE)))
    w_sorted = jnp.pad(w_sorted, ((0, 0), (0, EXPERT_TILE)))

    ids = jnp.arange(N_EXPERTS, dtype=jnp.int32)
    counts = jnp.sum(e[:, :, None] == ids[None, None, :], axis=1, dtype=jnp.int32)
    starts = jnp.cumsum(counts, axis=1) - counts
    tiles = (counts + EXPERT_TILE - 1) // EXPERT_TILE
    tile_end = jnp.cumsum(tiles, axis=1)
    ntiles = tile_end[:, -1:]
    tid = jnp.arange(n // EXPERT_TILE + N_EXPERTS + 2, dtype=jnp.int32)[None, :]
    tile_e = jnp.sum(tid[:, :, None] >= tile_end[:, None, :], axis=-1, dtype=jnp.int32)
    tile_e = jnp.minimum(tile_e, N_EXPERTS - 1)
    take = lambda a: jnp.take_along_axis(a, tile_e, axis=1)
    within = (tid - (take(tile_end) - take(tiles))) * EXPERT_TILE
    live = tid < ntiles
    tile_src = jnp.where(live, take(starts) + within, 0)
    tile_n = jnp.where(live, jnp.clip(take(counts) - within, 0, EXPERT_TILE), 0)
    return (tile_e, tile_src, tile_n, ntiles), tok, w_sorted


def kernel(x, c, ctx, c_ctx, w_mod, b_mod, norm_mix_g, w_in, q_norm_g, w_q_up, kv_norm_g, w_kv_up, conv_w, conv_b,
           lru_w_a, lru_b_a, lru_w_x, lru_b_x, lru_lambda, w_out, norm_ffn_g, router_w, router_bias, exp_w_gate,
           exp_w_up, exp_w_down, sh_w_gate, sh_w_up, sh_w_down, final_norm_g):
    b_, s_, d = x.shape
    nc = ctx.shape[1]
    assert w_mod.shape[0] == 1, "single-layer operation"
    assert s_ % GRID_W == 0 and s_ % SCAN_CHUNK == 0 and nc % SCAN_CHUNK == 0
    assert s_ % min(TOKEN_TILE, s_) == 0 and s_ % min(Q_TILE, s_) == 0

    wts = _prep_weights(w_in[0], q_norm_g[0], w_q_up[0], kv_norm_g[0], w_kv_up[0], lru_w_a[0], lru_b_a[0],
                        lru_w_x[0], lru_b_x[0], w_out[0], router_w[0], router_bias[0], exp_w_gate[0], exp_w_up[0],
                        exp_w_down[0], sh_w_gate[0], sh_w_up[0], sh_w_down[0])
    c_tab, s_tab, c_ctx_tab, s_ctx_tab = _rope_tables(s_, nc)

    rows = (b_ + 1 + SUBLANES - 1) // SUBLANES * SUBLANES
    cc = jnp.zeros((rows, d), F32).at[:b_].set(c).at[b_].set(c_ctx)
    mod = _modulation(cc, w_mod[0], b_mod[0][None])
    sh1, sc1, g1, sh2, sc2, g2 = [mod[:b_, i * d:(i + 1) * d].reshape(b_, 1, d) for i in range(6)]
    csh1, csc1 = [mod[b_, i * d:(i + 1) * d].reshape(1, 1, d) for i in range(2)]

    q, k_lat, v_lat, lx_lat, gl = _input_projection(
        x, sh1, sc1, norm_mix_g, wts, (c_tab * ATTN_SCALE, s_tab * ATTN_SCALE, c_tab, s_tab), True)
    k_ctx, v_ctx, lx_ctx, _ = _input_projection(
        ctx, csh1, csc1, norm_mix_g, wts, (c_ctx_tab, s_ctx_tab, c_ctx_tab, s_ctx_tab), False)

    o_lat = _attention(q, k_lat, k_ctx, v_lat, v_ctx)
    y2 = _rglru(lx_lat, lx_ctx, gl, conv_w[0], conv_b[0][None], wts["wg"], wts["bg"],
                lru_lambda[0].reshape(2, 1, LRU_WIDTH))

    xs, fp, eidx, rw = _output_projection(o_lat, y2, x, g1, sh2, sc2, g2, norm_ffn_g, wts)

    tables, tok, wl = _dispatch(eidx, rw, b_, s_)
    fp4 = fp.reshape(b_, s_ * (d // LANES), LANES)
    routed = [_routed_experts([tb[b] for tb in tables], tok[b], wl[b], fp4[b], wts["exp_gu"], wts["exp_d"], s_)
              for b in range(b_)]
    routed = jnp.stack(routed).reshape(b_, s_, d)

    return _final(xs, routed, g2, final_norm_g[None])
```

```python
import functools

import jax
import jax.numpy as jnp
from jax import lax
from jax.experimental import pallas as pl
from jax.experimental.pallas import tpu as pltpu

GRID_W = 64
EPS = 1e-6
MLA_HEADS = 8
MLA_NOPE = 64
MLA_ROPE = 32
MLA_V = 64
MLA_Q_RANK = 256
MLA_KV_RANK = 128
LRU_WIDTH = 512
LRU_BLOCKS = 8
CONV_W = 4
RG_C = 8.0
ROPE_BASE = 10000.0
ROPE_AXIS = MLA_ROPE // 2
ATTN_SCALE = (MLA_NOPE + MLA_ROPE) ** -0.5
N_EXPERTS = 64
TOP_K = 8
N_GROUPS = 8
TOPK_GROUPS = 4
EXPERT_FF = 256
ROUTED_SCALE = 2.5
COL_KV = MLA_Q_RANK
COL_KR = COL_KV + MLA_KV_RANK
COL_LRU_X = COL_KR + MLA_ROPE
COL_LRU_G = COL_LRU_X + LRU_WIDTH

LANES = 128
SUBLANES = 8
VMEM_LIMIT = 56 * 1024 * 1024

TOKEN_TILE = 512
Q_TILE = 512
KEY_CHUNK = 512
SCAN_CHUNK = 128
SCAN_GROUP = 4
EXPERT_TILE = 256
TILE_STRIDE = EXPERT_TILE + 1
SCATTER_UNROLL = 8

F32 = jnp.float32
BF16 = jnp.bfloat16
NEG_INF = float("-inf")


def _params(*sem):
    return pltpu.CompilerParams(dimension_semantics=sem, vmem_limit_bytes=VMEM_LIMIT)


def _rms(t, g):
    return t * lax.rsqrt(jnp.mean(t * t, axis=-1, keepdims=True) + EPS) * g


def _gelu_tanh(t):
    return 0.5 * t * (1.0 + jnp.tanh(0.7978845608028654 * (t + 0.044715 * (t * t * t))))


def _silu(t):
    return t * jax.nn.sigmoid(t)


def _rows_to_slabs(rows, stage_ref, slab_ref):
    n = rows.shape[0]
    st = n + 1
    for j in range(SUBLANES):
        stage_ref[j * st:j * st + n, :] = rows[:, j * LANES:(j + 1) * LANES]
    for r in range(n):
        slab_ref[r * SUBLANES:(r + 1) * SUBLANES, :] = stage_ref[pl.ds(r, SUBLANES, stride=st), :]


def _slabs_to_rows(slab_ref, stage_ref, n):
    st = n + 1
    for r in range(n):
        stage_ref[pl.ds(r, SUBLANES, stride=st), :] = slab_ref[r * SUBLANES:(r + 1) * SUBLANES, :]
    return jnp.concatenate([stage_ref[j * st:j * st + n, :] for j in range(SUBLANES)], axis=-1)


def _nt_dot(a, b):
    return lax.dot_general(a, b, (((1,), (1,)), ((), ())), preferred_element_type=F32)


def _mod_kernel(c_ref, w_ref, b_ref, o_ref):
    s = _silu(c_ref[...])
    o_ref[...] = jnp.dot(s, w_ref[...], precision=lax.Precision.HIGHEST,
                         preferred_element_type=F32) + b_ref[...]


def _modulation(cc, w, b):
    rows, d = cc.shape
    n = w.shape[1]
    tn = 1024
    return pl.pallas_call(
        _mod_kernel,
        out_shape=jax.ShapeDtypeStruct((rows, n), F32),
        grid=(n // tn,),
        in_specs=[pl.BlockSpec((rows, d), lambda j: (0, 0)),
                  pl.BlockSpec((d, tn), lambda j: (0, j)),
                  pl.BlockSpec((1, tn), lambda j: (0, j))],
        out_specs=pl.BlockSpec((rows, tn), lambda j: (0, j)),
        compiler_params=_params("arbitrary"),
        name="modulation",
    )(cc, w, b)


def _inproj_kernel(*refs, with_q):
    (x_ref, sh_ref, sc_ref, g_ref, win_ref, qg_ref, kvg_ref, wq_ref, wqs_ref, wk_ref, wks_ref,
     wv_ref, vo_ref, cq_ref, sq_ref, ck_ref, sk_ref) = refs[:17]
    if with_q:
        q_ref, k_ref, v_ref, lx_ref, lg_ref = refs[17:]
    else:
        k_ref, v_ref, lx_ref, lg_ref = refs[17:]
    h = _rms(x_ref[0], g_ref[...]) * (1.0 + sc_ref[0]) + sh_ref[0]
    p = jnp.dot(h.astype(BF16), win_ref[...], preferred_element_type=F32)

    if with_q:
        qn = _rms(p[:, :MLA_Q_RANK], qg_ref[...]).astype(BF16)
        qa = jnp.dot(qn, wq_ref[...], preferred_element_type=F32)
        qb = jnp.dot(qn, wqs_ref[...], preferred_element_type=F32)
        cq, sq = cq_ref[...], sq_ref[...]
        for hd in range(MLA_HEADS):
            sl = slice(hd * LANES, (hd + 1) * LANES)
            q_ref[0, :, sl] = (qa[:, sl] * cq + qb[:, sl] * sq).astype(BF16)

    kvn = _rms(p[:, COL_KV:COL_KR], kvg_ref[...])
    kvn16 = kvn.astype(BF16)
    kin = jnp.concatenate([kvn16, p[:, COL_KR:COL_KR + LANES].astype(BF16)], axis=-1)
    ka = jnp.dot(kin, wk_ref[...], preferred_element_type=F32)
    kb = jnp.dot(kin, wks_ref[...], preferred_element_type=F32)
    ck, sk = ck_ref[...], sk_ref[...]
    for hd in range(MLA_HEADS):
        sl = slice(hd * LANES, (hd + 1) * LANES)
        k_ref[0, :, sl] = (ka[:, sl] * ck + kb[:, sl] * sk).astype(BF16)
    v_ref[0] = (jnp.dot(kvn16, wv_ref[...], preferred_element_type=F32) + vo_ref[...]).astype(BF16)
    lx_ref[0] = p[:, 512:512 + LRU_WIDTH]
    lg_ref[0] = _gelu_tanh(p[:, 1024:1024 + LRU_WIDTH]).astype(BF16)


def _input_projection(x, sh, sc, g, wts, tabs, with_q):
    b_, s_, d = x.shape
    tm = min(TOKEN_TILE, s_)
    hw = MLA_HEADS * LANES
    per_batch = sh.shape[0] == b_
    mod_spec = pl.BlockSpec((1, 1, d), (lambda b, i: (b, 0, 0)) if per_batch else (lambda b, i: (0, 0, 0)))

    def const(a):
        return pl.BlockSpec(a.shape, lambda b, i: (0,) * a.ndim)

    tab_spec = pl.BlockSpec((tm, LANES), lambda b, i: (i, 0))
    wide = lambda w: pl.BlockSpec((1, tm, w), lambda b, i: (b, i, 0))
    weights = (wts["w_in"], wts["q_g"], wts["kv_g"], wts["wq"], wts["wq_sw"], wts["wk"], wts["wk_sw"], wts["wv"],
               wts["v_ones"])
    out_shape = [jax.ShapeDtypeStruct((b_, s_, hw), BF16),
                 jax.ShapeDtypeStruct((b_, s_, hw), BF16),
                 jax.ShapeDtypeStruct((b_, s_, LRU_WIDTH), F32),
                 jax.ShapeDtypeStruct((b_, s_, LRU_WIDTH), BF16)]
    out_specs = [wide(hw), wide(hw), wide(LRU_WIDTH), wide(LRU_WIDTH)]
    if with_q:
        out_shape = [jax.ShapeDtypeStruct((b_, s_, hw), BF16)] + out_shape
        out_specs = [wide(hw)] + out_specs
    return pl.pallas_call(
        functools.partial(_inproj_kernel, with_q=with_q),
        out_shape=out_shape,
        grid=(b_, s_ // tm),
        in_specs=[wide(d), mod_spec, mod_spec, const(g)] + [const(w) for w in weights] + [tab_spec] * 4,
        out_specs=out_specs,
        compiler_params=_params("parallel", "parallel"),
        name="input_projection_lat" if with_q else "input_projection_ctx",
    )(x, sh, sc, g, *weights, *tabs)


def _attn_kernel(q_ref, kl_ref, kc_ref, vl_ref, vc_ref, o_ref):
    s_ = kl_ref.shape[1]
    kc = min(KEY_CHUNK, s_)
    half = LANES // 2
    lane = lax.broadcasted_iota(jnp.int32, (q_ref.shape[1], LANES), 1)
    chunks = [(kl_ref, vl_ref, c * kc, kc) for c in range(s_ // kc)] + [(kc_ref, vc_ref, 0, kc_ref.shape[1])]
    items = [(a, ch) for a in range(2) for ch in chunks]

    def scores(item):
        a, (k_ref, _, start, size) = item
        sl = slice(a * LANES, (a + 1) * LANES)
        return _nt_dot(q_ref[0, :, sl], k_ref[0, start:start + size, sl])

    out = None
    s_next = scores(items[0])
    for i, (a, (_, v_ref, start, size)) in enumerate(items):
        s = s_next
        if i + 1 < len(items):
            s_next = scores(items[i + 1])
        first = i % len(chunks) == 0
        m_c = jnp.max(s, axis=-1, keepdims=True)
        m_new = m_c if first else jnp.maximum(m, m_c)
        p = jnp.exp(s - m_new).astype(BF16)
        pv = jnp.dot(p, v_ref[0, start:start + size, a * LANES:(a + 1) * LANES], preferred_element_type=F32)
        acc = pv if first else acc * jnp.exp(m - m_new) + pv
        m = m_new
        if (i + 1) % len(chunks) == 0:
            own = (lane < half) if a == 0 else (lane >= half)
            row_sum = pltpu.roll(acc, half, 1)
            o = jnp.where(own, acc / row_sum, 0.0)
            out = o if out is None else out + o
    o_ref[0] = out.astype(BF16)


def _attention(q, k_lat, k_ctx, v_lat, v_ctx):
    b_, s_, hw = q.shape
    nc = k_ctx.shape[1]
    tq = min(Q_TILE, s_)
    pair = 2 * LANES
    n_pairs = hw // pair
    return pl.pallas_call(
        _attn_kernel,
        out_shape=jax.ShapeDtypeStruct((b_, s_, n_pairs * LANES), BF16),
        grid=(b_, n_pairs, s_ // tq),
        in_specs=[pl.BlockSpec((1, tq, pair), lambda b, j, i: (b, i, j)),
                  pl.BlockSpec((1, s_, pair), lambda b, j, i: (b, 0, j)),
                  pl.BlockSpec((1, nc, pair), lambda b, j, i: (b, 0, j)),
                  pl.BlockSpec((1, s_, pair), lambda b, j, i: (b, 0, j)),
                  pl.BlockSpec((1, nc, pair), lambda b, j, i: (b, 0, j))],
        out_specs=pl.BlockSpec((1, tq, LANES), lambda b, j, i: (b, i, j)),
        compiler_params=_params("parallel", "parallel", "arbitrary"),
        name="attention",
    )(q, k_lat, k_ctx, v_lat, v_ctx)


def _scan_chunk(a, b, reverse):
    tc = a.shape[0]
    row = lax.broadcasted_iota(jnp.int32, a.shape, 0)
    s = 1
    while s < tc:
        if s < SUBLANES:
            if reverse:
                keep = row < tc - s
                a_sh = jnp.where(keep, pltpu.roll(a, tc - s, 0), 1.0)
                b_sh = jnp.where(keep, pltpu.roll(b, tc - s, 0), 0.0)
            else:
                keep = row >= s
                a_sh = jnp.where(keep, pltpu.roll(a, s, 0), 1.0)
                b_sh = jnp.where(keep, pltpu.roll(b, s, 0), 0.0)
        else:
            ones = jnp.ones((s, a.shape[1]), F32)
            zeros = jnp.zeros((s, a.shape[1]), F32)
            if reverse:
                a_sh = jnp.concatenate([a[s:], ones], axis=0)
                b_sh = jnp.concatenate([b[s:], zeros], axis=0)
            else:
                a_sh = jnp.concatenate([ones, a[:tc - s]], axis=0)
                b_sh = jnp.concatenate([zeros, b[:tc - s]], axis=0)
        b = a * b_sh + b
        a = a * a_sh
        s *= 2
    return a, b


def _lru_kernel(xl_ref, xc_ref, gl_ref, cw_ref, cb_ref, wg_ref, bg_ref, lam_ref, o_ref,
                padl_ref, padc_ref, hf_ref, hb_ref):
    s_ = xl_ref.shape[1]
    nc = xc_ref.shape[1]
    tc = SCAN_CHUNK
    halo = SUBLANES
    zero_halo = jnp.zeros((halo, LANES), F32)
    padl_ref[0:halo, :] = zero_halo
    padl_ref[halo:halo + s_, :] = xl_ref[0]
    padl_ref[halo + s_:2 * halo + s_, :] = zero_halo
    padc_ref[0:halo, :] = zero_halo
    padc_ref[halo:halo + nc, :] = xc_ref[0]
    padc_ref[halo + nc:2 * halo + nc, :] = zero_halo

    cw = cw_ref[...]
    cb = cb_ref[...]

    def coeffs(pad_ref, j, d):
        win = pad_ref[pl.ds(pl.multiple_of(j * tc, SUBLANES), tc + 2 * halo), :]
        n = tc + 2 * halo
        u = (cb + cw[0:1] * pltpu.roll(win, 2, 0)[halo:halo + tc]
             + cw[1:2] * pltpu.roll(win, 1, 0)[halo:halo + tc]
             + cw[2:3] * win[halo:halo + tc]
             + cw[3:4] * pltpu.roll(win, n - 1, 0)[halo:halo + tc])
        gates = jnp.dot(u.astype(BF16), wg_ref[d, 0], preferred_element_type=F32) + bg_ref[d, 0]
        r = jax.nn.sigmoid(gates[:, :LANES])
        i = jax.nn.sigmoid(gates[:, LANES:])
        z = -lam_ref[d]
        softplus = jnp.maximum(z, 0.0) + jnp.log(1.0 + jnp.exp(-jnp.abs(z)))
        log_a = (-RG_C) * r * softplus
        a = jnp.exp(log_a)
        b = jnp.sqrt(1.0 - a * a) * (i * u)
        return a, b

    def local_scan(pad_ref, j, d):
        a, b = coeffs(pad_ref, j, d)
        return _scan_chunk(a, b, reverse=(d == 1))

    def apply_carry(scanned, d, carry):
        a_cum, b_loc = scanned
        h = a_cum * carry + b_loc
        return h, (h[0:1] if d == 1 else h[tc - 1:tc])

    cf = cb_ = jnp.zeros((1, LANES), F32)
    n_ctx = nc // tc
    for j in range(n_ctx):
        _, cf = apply_carry(local_scan(padc_ref, j, 0), 0, cf)
        _, cb_ = apply_carry(local_scan(padc_ref, n_ctx - 1 - j, 1), 1, cb_)

    n_lat = s_ // tc
    grp = min(SCAN_GROUP, n_lat)

    def body(jj, carry):
        cf, cb_ = carry
        fwd = [jj * grp + g for g in range(grp)]
        bwd = [n_lat - 1 - jj * grp - g for g in range(grp)]
        scans_f = [local_scan(padl_ref, j, 0) for j in fwd]
        scans_b = [local_scan(padl_ref, j, 1) for j in bwd]
        for j, sc in zip(fwd, scans_f):
            h, cf = apply_carry(sc, 0, cf)
            hf_ref[pl.ds(pl.multiple_of(j * tc, tc), tc), :] = h
        for j, sc in zip(bwd, scans_b):
            h, cb_ = apply_carry(sc, 1, cb_)
            hb_ref[pl.ds(pl.multiple_of(j * tc, tc), tc), :] = h
        return cf, cb_

    lax.fori_loop(0, n_lat // grp, body, (cf, cb_))
    o_ref[0] = ((hf_ref[...] + hb_ref[...]) * gl_ref[0].astype(F32)).astype(BF16)


def _rglru(lx_lat, lx_ctx, gl, conv_w, conv_b, wg, bg, lam):
    b_, s_, w = lx_lat.shape
    nc = lx_ctx.shape[1]
    ng = w // LANES
    halo = SUBLANES
    return pl.pallas_call(
        _lru_kernel,
        out_shape=jax.ShapeDtypeStruct((b_, s_, w), BF16),
        grid=(b_, ng),
        in_specs=[pl.BlockSpec((1, s_, LANES), lambda b, g: (b, 0, g)),
                  pl.BlockSpec((1, nc, LANES), lambda b, g: (b, 0, g)),
                  pl.BlockSpec((1, s_, LANES), lambda b, g: (b, 0, g)),
                  pl.BlockSpec((CONV_W, LANES), lambda b, g: (0, g)),
                  pl.BlockSpec((1, LANES), lambda b, g: (0, g)),
                  pl.BlockSpec((2, 1, LANES, 2 * LANES), lambda b, g: (0, g, 0, 0)),
                  pl.BlockSpec((2, 1, 1, 2 * LANES), lambda b, g: (0, g, 0, 0)),
                  pl.BlockSpec((2, 1, LANES), lambda b, g: (0, 0, g))],
        out_specs=pl.BlockSpec((1, s_, LANES), lambda b, g: (b, 0, g)),
        scratch_shapes=[pltpu.VMEM((s_ + 2 * halo, LANES), F32),
                        pltpu.VMEM((nc + 2 * halo, LANES), F32),
                        pltpu.VMEM((s_, LANES), F32),
                        pltpu.VMEM((s_, LANES), F32)],
        compiler_params=_params("parallel", "parallel"),
        name="rglru",
    )(lx_lat, lx_ctx, gl, conv_w, conv_b, wg, bg, lam)


def _first_index(mask, iota, limit, axis):
    return jnp.min(jnp.where(mask, iota, limit), axis=axis, keepdims=True)


def _outproj_kernel(o_ref, y2_ref, x_ref, g1_ref, sh_ref, sc_ref, g2_ref, gf_ref, wo_ref, wrt_ref, rb_ref,
                    wsgu_ref, wsd_ref, xs_ref, fp_ref, e_ref, w_ref, stage_ref):
    half = wo_ref.shape[0] // 2
    mix = (jnp.dot(o_ref[0], wo_ref[0:half, :], preferred_element_type=F32)
           + jnp.dot(y2_ref[0], wo_ref[half:, :], preferred_element_type=F32))
    x1 = x_ref[0] + g1_ref[0] * mix
    f = _rms(x1, gf_ref[...]) * (1.0 + sc_ref[0]) + sh_ref[0]
    f16 = f.astype(BF16)
    tm = f.shape[0]

    logits = lax.dot_general(wrt_ref[...], f, (((1,), (1,)), ((), ())),
                             precision=lax.Precision.HIGHEST, preferred_element_type=F32)
    scores = jax.nn.sigmoid(logits)
    sel = scores + rb_ref[...]
    per = N_EXPERTS // N_GROUPS
    g3 = sel.reshape(N_GROUPS, per, tm)
    mem = lax.broadcasted_iota(jnp.int32, g3.shape, 1)
    m1 = jnp.max(g3, axis=1, keepdims=True)
    first = _first_index(g3 == m1, mem, per, 1)
    m2 = jnp.max(jnp.where(mem == first, NEG_INF, g3), axis=1, keepdims=True)
    gscore = (m1 + m2).reshape(N_GROUPS, tm)
    giota = lax.broadcasted_iota(jnp.int32, gscore.shape, 0)
    gmask = jnp.zeros(gscore.shape, F32)
    cur = gscore
    for _ in range(TOPK_GROUPS):
        mx = jnp.max(cur, axis=0, keepdims=True)
        pick = giota == _first_index(cur == mx, giota, N_GROUPS, 0)
        gmask = jnp.where(pick, 1.0, gmask)
        cur = jnp.where(pick, NEG_INF, cur)
    allowed = jnp.broadcast_to(gmask.reshape(N_GROUPS, 1, tm), g3.shape) > 0.0
    cur = jnp.where(allowed, g3, NEG_INF).reshape(N_EXPERTS, tm)
    eiota = lax.broadcasted_iota(jnp.int32, cur.shape, 0)
    picked_w = []
    for k in range(TOP_K):
        mx = jnp.max(cur, axis=0, keepdims=True)
        idx = _first_index(cur == mx, eiota, N_EXPERTS, 0)
        pick = eiota == idx
        e_ref[k:k + 1, :] = idx
        picked_w.append(jnp.sum(jnp.where(pick, scores, 0.0), axis=0, keepdims=True))
        cur = jnp.where(pick, NEG_INF, cur)
    wsum = picked_w[0]
    for k in range(1, TOP_K):
        wsum = wsum + picked_w[k]
    for k in range(TOP_K):
        w_ref[k:k + 1, :] = ROUTED_SCALE * picked_w[k] / wsum

    a = jnp.dot(f16, wsgu_ref[...], preferred_element_type=F32)
    ff = a.shape[1] // 2
    act = _silu(a[:, :ff]) * a[:, ff:]
    shared = jnp.dot(act.astype(BF16), wsd_ref[...], preferred_element_type=F32)
    xs_ref[0] = x1 + g2_ref[0] * shared
    _rows_to_slabs(f, stage_ref, fp_ref.at[0])


def _output_projection(o, y2, x, g1, sh2, sc2, g2, gf, wts):
    b_, s_, d = x.shape
    tm = min(TOKEN_TILE, s_)
    nt = s_ // tm
    mod_spec = pl.BlockSpec((1, 1, d), lambda b, i: (b, 0, 0))

    def const(a):
        return pl.BlockSpec(a.shape, lambda b, i: (0,) * a.ndim)

    wide = lambda w: pl.BlockSpec((1, tm, w), lambda b, i: (b, i, 0))
    route_spec = pl.BlockSpec((TOP_K, tm), lambda b, i: (0, b * nt + i))
    weights = (wts["w_out"], wts["router_t"], wts["router_b"], wts["sh_gu"], wts["sh_d"])
    return pl.pallas_call(
        _outproj_kernel,
        out_shape=[jax.ShapeDtypeStruct((b_, s_, d), F32),
                   jax.ShapeDtypeStruct((b_, s_ * SUBLANES, LANES), F32),
                   jax.ShapeDtypeStruct((TOP_K, b_ * s_), jnp.int32),
                   jax.ShapeDtypeStruct((TOP_K, b_ * s_), F32)],
        grid=(b_, nt),
        in_specs=[wide(o.shape[2]), wide(y2.shape[2]), wide(d), mod_spec, mod_spec, mod_spec, mod_spec,
                  const(gf)] + [const(w) for w in weights],
        out_specs=[wide(d), pl.BlockSpec((1, tm * SUBLANES, LANES), lambda b, i: (b, i, 0)), route_spec, route_spec],
        scratch_shapes=[pltpu.VMEM((SUBLANES * (tm + 1), LANES), F32)],
        compiler_params=_params("parallel", "parallel"),
        name="output_projection",
    )(o, y2, x, g1, sh2, sc2, g2, gf, *weights)


def _moe_kernel(tile_e_ref, tile_src_ref, tile_n_ref, ntiles_ref, tok_ref, wl_ref,
                fp_ref, wgu_ref, wd_ref, acc_ref, tin_a, tin_b, tout_a, tout_b):
    t = pl.program_id(0)
    m = EXPERT_TILE
    st = TILE_STRIDE
    n_out = tout_a.shape[0] // st

    def slab(off):
        return pl.ds(pl.multiple_of(off, SUBLANES), SUBLANES)

    def gather(tile, tin):
        base = tile_src_ref[tile]
        for mi in range(m):
            tin[pl.ds(mi, SUBLANES, stride=st), :] = fp_ref[slab(tok_ref[base + mi]), :]

    def experts(tile, tin, tout):
        xt = jnp.concatenate([tin[j * st:j * st + m, :].astype(BF16) for j in range(n_out)], axis=-1)
        hcat = jnp.dot(xt, wgu_ref[0], preferred_element_type=F32)
        ff = hcat.shape[1] // 2
        act = _silu(hcat[:, :ff]) * hcat[:, ff:]
        y = jnp.dot(act.astype(BF16), wd_ref[0], preferred_element_type=F32)
        valid = lax.broadcasted_iota(jnp.int32, (m, LANES), 0) < tile_n_ref[tile]
        for j in range(n_out):
            tout[j * st:j * st + m, :] = jnp.where(valid, y[:, j * LANES:(j + 1) * LANES], 0.0)

    def scatter(tile, tout):
        base = tile_src_ref[tile]
        for g in range(m // SCATTER_UNROLL):
            pend = []
            for r in range(SCATTER_UNROLL):
                row = g * SCATTER_UNROLL + r
                dst = slab(tok_ref[base + row])
                pend.append((dst, acc_ref[dst, :] + wl_ref[base + row] * tout[pl.ds(row, SUBLANES, stride=st), :]))
            for dst, new in reversed(pend):
                acc_ref[dst, :] = new

    @pl.when(t == 0)
    def _():
        acc_ref[...] = jnp.zeros(acc_ref.shape, F32)
        tout_b[...] = jnp.zeros(tout_b.shape, F32)
        gather(0, tin_a)

    def step(tin_cur, tin_nxt, tout_cur, tout_prev):
        gather(t + 1, tin_nxt)
        experts(t, tin_cur, tout_cur)
        scatter(jnp.maximum(t - 1, 0), tout_prev)

    active = t <= ntiles_ref[0]

    @pl.when(active & (t % 2 == 0))
    def _():
        step(tin_a, tin_b, tout_a, tout_b)

    @pl.when(active & (t % 2 == 1))
    def _():
        step(tin_b, tin_a, tout_b, tout_a)


def _moe_kernel_chained(*refs):
    _moe_kernel(*refs[:9], *refs[10:])


def _routed_experts(tables, tok, wl, fp, wgu, wd, b, routed_prev):
    tile_e, tile_src, tile_n, ntiles = tables
    d = wd.shape[2]
    n_b, rows, _ = fp.shape
    assert d == SUBLANES * LANES
    stage = d // LANES * TILE_STRIDE
    n_steps = tile_e.shape[0] - 1
    in_specs = [pl.BlockSpec((None, rows, LANES), lambda t, *_: (b, 0, 0), pipeline_mode=pl.Buffered(1)),
                pl.BlockSpec((1,) + wgu.shape[1:], lambda t, te, *_: (te[t], 0, 0)),
                pl.BlockSpec((1,) + wd.shape[1:], lambda t, te, *_: (te[t], 0, 0))]
    operands = [tile_e, tile_src, tile_n, ntiles, tok, wl, fp, wgu, wd]
    aliases = {}
    if routed_prev is not None:
        in_specs.append(pl.BlockSpec(memory_space=pl.ANY))
        operands.append(routed_prev)
        aliases = {len(operands) - 1: 0}
    grid_spec = pltpu.PrefetchScalarGridSpec(
        num_scalar_prefetch=6,
        grid=(n_steps,),
        in_specs=in_specs,
        out_specs=pl.BlockSpec((rows, LANES), lambda t, *_: (b, 0), pipeline_mode=pl.Buffered(1)),
        scratch_shapes=[pltpu.VMEM((stage, LANES), F32) for _ in range(4)],
    )
    return pl.pallas_call(
        _moe_kernel if routed_prev is None else _moe_kernel_chained,
        out_shape=jax.ShapeDtypeStruct((n_b * rows, LANES), F32),
        grid_spec=grid_spec,
        input_output_aliases=aliases,
        compiler_params=_params("arbitrary"),
        name="routed_experts",
    )(*operands)


def _final_kernel(xs_ref, r_ref, g2_ref, g_ref, o_ref, stage_ref):
    routed = _slabs_to_rows(r_ref, stage_ref, xs_ref.shape[1])
    o_ref[0] = _rms(xs_ref[0] + g2_ref[0] * routed, g_ref[...])


def _final(xs, routed, g2, g):
    b_, s_, d = xs.shape
    tm = min(TOKEN_TILE, s_)
    nt = s_ // tm
    wide = pl.BlockSpec((1, tm, d), lambda b, i: (b, i, 0))
    return pl.pallas_call(
        _final_kernel,
        out_shape=jax.ShapeDtypeStruct((b_, s_, d), F32),
        grid=(b_, nt),
        in_specs=[wide, pl.BlockSpec((tm * SUBLANES, LANES), lambda b, i: (b * nt + i, 0)),
                  pl.BlockSpec((1, 1, d), lambda b, i: (b, 0, 0)),
                  pl.BlockSpec((1, d), lambda b, i: (0, 0))],
        out_specs=wide,
        scratch_shapes=[pltpu.VMEM((SUBLANES * (tm + 1), LANES), F32)],
        compiler_params=_params("parallel", "parallel"),
        name="final_norm",
    )(xs, routed, g2, g)


def _prep_weights(w_in, q_norm_g, w_q_up, kv_norm_g, w_kv_up, lru_w_a, lru_b_a, lru_w_x, lru_b_x, w_out,
                  router_w, router_bias, exp_w_gate, exp_w_up, exp_w_down, sh_w_gate, sh_w_up, sh_w_down):
    d = w_in.shape[0]
    h_ = MLA_HEADS
    pad_kr = jnp.zeros((d, LANES - MLA_ROPE), F32)
    w_in_p = jnp.concatenate([w_in[:, :COL_LRU_X], pad_kr, w_in[:, COL_LRU_X:]], axis=1).astype(BF16)

    wq = w_q_up.reshape(MLA_Q_RANK, h_, MLA_NOPE + MLA_ROPE)
    nope, rope = wq[:, :, :MLA_NOPE], wq[:, :, MLA_NOPE:]
    rope_sw = rope.reshape(MLA_Q_RANK, h_, 2, 2, ROPE_AXIS // 2)[:, :, :, ::-1, :].reshape(rope.shape)
    zpad = jnp.zeros((MLA_Q_RANK, h_, LANES - MLA_NOPE - MLA_ROPE), F32)
    wq_p = jnp.concatenate([nope, rope, zpad], axis=-1).reshape(MLA_Q_RANK, h_ * LANES).astype(BF16)
    wq_sw = jnp.concatenate([jnp.zeros_like(nope), rope_sw, zpad], axis=-1).reshape(MLA_Q_RANK, h_ * LANES).astype(BF16)

    wkv = w_kv_up.reshape(MLA_KV_RANK, h_, MLA_NOPE + MLA_V)
    k_nope, v_w = wkv[:, :, :MLA_NOPE], wkv[:, :, MLA_NOPE:]
    r_idx = jnp.arange(MLA_ROPE)
    place = jnp.zeros((LANES, h_, LANES), F32).at[r_idx, :, MLA_NOPE + r_idx].set(1.0)
    place_sw = jnp.zeros((LANES, h_, LANES), F32).at[r_idx ^ (ROPE_AXIS // 2), :, MLA_NOPE + r_idx].set(1.0)
    k_top = jnp.concatenate([k_nope, jnp.zeros((MLA_KV_RANK, h_, LANES - MLA_NOPE), F32)], axis=-1)
    wk = jnp.concatenate([k_top, place], axis=0).reshape(MLA_KV_RANK + LANES, h_ * LANES).astype(BF16)
    wk_sw = jnp.concatenate([jnp.zeros_like(k_top), place_sw], axis=0).reshape(MLA_KV_RANK + LANES, h_ * LANES).astype(BF16)
    zv = jnp.zeros_like(v_w)
    even = (jnp.arange(h_) % 2 == 0)[None, :, None]
    wv = jnp.concatenate([jnp.where(even, v_w, zv), jnp.where(even, zv, v_w)], axis=-1)
    wv = wv.reshape(MLA_KV_RANK, h_ * LANES).astype(BF16)
    one_lo = jnp.concatenate([jnp.zeros((MLA_V,), F32), jnp.ones((LANES - MLA_V,), F32)])
    one_hi = jnp.concatenate([jnp.ones((LANES - MLA_V,), F32), jnp.zeros((MLA_V,), F32)])
    v_ones = jnp.where(even[0], one_lo[None, :], one_hi[None, :]).reshape(1, h_ * LANES)

    eye = jnp.eye(LRU_BLOCKS, dtype=F32)
    ng = LRU_WIDTH // LANES

    def dense(w):
        return jnp.einsum("xncd,nm->xncmd", w, eye).reshape(2, LRU_WIDTH, LRU_WIDTH)

    def grp(wd_):
        return jnp.stack([wd_[:, g * LANES:(g + 1) * LANES, g * LANES:(g + 1) * LANES] for g in range(ng)], axis=1)

    wg = jnp.concatenate([grp(dense(lru_w_a)), grp(dense(lru_w_x))], axis=-1).astype(BF16)
    bg = jnp.concatenate([lru_b_a.reshape(2, ng, 1, LANES), lru_b_x.reshape(2, ng, 1, LANES)], axis=-1)

    return dict(
        w_in=w_in_p, q_g=q_norm_g[None], kv_g=kv_norm_g[None], wq=wq_p, wq_sw=wq_sw, wk=wk, wk_sw=wk_sw, wv=wv,
        v_ones=v_ones,
        wg=wg, bg=bg,
        w_out=w_out.astype(BF16), router_t=router_w.T, router_b=router_bias[:, None],
        sh_gu=jnp.concatenate([sh_w_gate, sh_w_up], axis=1).astype(BF16), sh_d=sh_w_down.astype(BF16),
        exp_gu=jnp.concatenate([exp_w_gate, exp_w_up], axis=2).astype(BF16), exp_d=exp_w_down.astype(BF16),
    )


def _rope_tables(s_, nc):
    rows = s_ // GRID_W
    row = jnp.repeat(jnp.arange(rows, dtype=F32), GRID_W)
    col = jnp.tile(jnp.arange(GRID_W, dtype=F32), rows)
    inv_freq = ROPE_BASE ** (-jnp.arange(0, ROPE_AXIS, 2, dtype=F32) / ROPE_AXIS)
    ang = jnp.stack([row, col], axis=-1)[:, :, None] * inv_freq
    cos = jnp.broadcast_to(jnp.cos(ang)[:, :, None, :], (s_, 2, 2, ROPE_AXIS // 2)).reshape(s_, MLA_ROPE)
    sin = jnp.sin(ang)[:, :, None, :] * jnp.array([-1.0, 1.0], F32)[None, None, :, None]
    sin = sin.reshape(s_, MLA_ROPE)
    ones = jnp.ones((s_, MLA_NOPE), F32)
    zeros = jnp.zeros((s_, LANES - MLA_NOPE - MLA_ROPE), F32)
    c_tab = jnp.concatenate([ones, cos, zeros], axis=1)
    s_tab = jnp.concatenate([jnp.zeros_like(ones), sin, zeros], axis=1)
    c_ctx = jnp.concatenate([jnp.ones((nc, MLA_NOPE + MLA_ROPE), F32),
                             jnp.zeros((nc, LANES - MLA_NOPE - MLA_ROPE), F32)], axis=1)
    return c_tab, s_tab, c_ctx, jnp.zeros_like(c_ctx)


def _dispatch(eidx, wts, b_, s_):
    n = TOP_K * s_
    e = eidx.reshape(TOP_K, b_, s_).transpose(1, 0, 2).reshape(b_, n)
    w = wts.reshape(TOP_K, b_, s_).transpose(1, 0, 2).reshape(b_, n)
    key = e * n + jnp.arange(n, dtype=jnp.int32)[None, :]
    key_sorted, w_sorted = lax.sort((key, w), dimension=1, num_keys=1)
    tok = ((key_sorted % n) % s_) * SUBLANES
    tok = jnp.pad(tok.astype(jnp.int32), ((0, 0), (0, EXPERT_TILE)))
    w_sorted = jnp.pad(w_sorted, ((0, 0), (0, EXPERT_TILE)))

    ids = jnp.arange(N_EXPERTS, dtype=jnp.int32)
    counts = jnp.sum(e[:, :, None] == ids[None, None, :], axis=1, dtype=jnp.int32)
    starts = jnp.cumsum(counts, axis=1) - counts
    tiles = (counts + EXPERT_TILE - 1) // EXPERT_TILE
    tile_end = jnp.cumsum(tiles, axis=1)
    ntiles = tile_end[:, -1:]
    tid = jnp.arange(n // EXPERT_TILE + N_EXPERTS + 2, dtype=jnp.int32)[None, :]
    tile_e = jnp.sum(tid[:, :, None] >= tile_end[:, None, :], axis=-1, dtype=jnp.int32)
    tile_e = jnp.minimum(tile_e, N_EXPERTS - 1)
    take = lambda a: jnp.take_along_axis(a, tile_e, axis=1)
    within = (tid - (take(tile_end) - take(tiles))) * EXPERT_TILE
    live = tid < ntiles
    tile_src = jnp.where(live, take(starts) + within, 0)
    tile_n = jnp.where(live, jnp.clip(take(counts) - within, 0, EXPERT_TILE), 0)
    return (tile_e, tile_src, tile_n, ntiles), tok, w_sorted


def kernel(x, c, ctx, c_ctx, w_mod, b_mod, norm_mix_g, w_in, q_norm_g, w_q_up, kv_norm_g, w_kv_up, conv_w, conv_b,
           lru_w_a, lru_b_a, lru_w_x, lru_b_x, lru_lambda, w_out, norm_ffn_g, router_w, router_bias, exp_w_gate,
           exp_w_up, exp_w_down, sh_w_gate, sh_w_up, sh_w_down, final_norm_g):
    b_, s_, d = x.shape
    nc = ctx.shape[1]
    assert w_mod.shape[0] == 1, "single-layer operation"
    assert s_ % GRID_W == 0 and s_ % SCAN_CHUNK == 0 and nc % SCAN_CHUNK == 0
    assert s_ % min(TOKEN_TILE, s_) == 0 and s_ % min(Q_TILE, s_) == 0

    wts = _prep_weights(w_in[0], q_norm_g[0], w_q_up[0], kv_norm_g[0], w_kv_up[0], lru_w_a[0], lru_b_a[0],
                        lru_w_x[0], lru_b_x[0], w_out[0], router_w[0], router_bias[0], exp_w_gate[0], exp_w_up[0],
                        exp_w_down[0], sh_w_gate[0], sh_w_up[0], sh_w_down[0])
    c_tab, s_tab, c_ctx_tab, s_ctx_tab = _rope_tables(s_, nc)

    rows = (b_ + 1 + SUBLANES - 1) // SUBLANES * SUBLANES
    cc = jnp.zeros((rows, d), F32).at[:b_].set(c).at[b_].set(c_ctx)
    mod = _modulation(cc, w_mod[0], b_mod[0][None])
    sh1, sc1, g1, sh2, sc2, g2 = [mod[:b_, i * d:(i + 1) * d].reshape(b_, 1, d) for i in range(6)]
    csh1, csc1 = [mod[b_, i * d:(i + 1) * d].reshape(1, 1, d) for i in range(2)]

    q, k_lat, v_lat, lx_lat, gl = _input_projection(
        x, sh1, sc1, norm_mix_g, wts, (c_tab * ATTN_SCALE, s_tab * ATTN_SCALE, c_tab, s_tab), True)
    k_ctx, v_ctx, lx_ctx, _ = _input_projection(
        ctx, csh1, csc1, norm_mix_g, wts, (c_ctx_tab, s_ctx_tab, c_ctx_tab, s_ctx_tab), False)

    o_lat = _attention(q, k_lat, k_ctx, v_lat, v_ctx)
    y2 = _rglru(lx_lat, lx_ctx, gl, conv_w[0], conv_b[0][None], wts["wg"], wts["bg"],
                lru_lambda[0].reshape(2, 1, LRU_WIDTH))

    xs, fp, eidx, rw = _output_projection(o_lat, y2, x, g1, sh2, sc2, g2, norm_ffn_g, wts)

    tables, tok, wl = _dispatch(eidx, rw, b_, s_)
    routed = None
    for b in range(b_):
        routed = _routed_experts([tb[b] for tb in tables], tok[b], wl[b], fp, wts["exp_gu"], wts["exp_d"], b, routed)

    return _final(xs, routed, g2, final_norm_g[None])
```

```python
import functools

import jax
import jax.numpy as jnp
from jax import lax
from jax.experimental import pallas as pl
from jax.experimental.pallas import tpu as pltpu

GRID_W = 64
EPS = 1e-6
MLA_HEADS = 8
MLA_NOPE = 64
MLA_ROPE = 32
MLA_V = 64
MLA_Q_RANK = 256
MLA_KV_RANK = 128
LRU_WIDTH = 512
LRU_BLOCKS = 8
CONV_W = 4
RG_C = 8.0
ROPE_BASE = 10000.0
ROPE_AXIS = MLA_ROPE // 2
ATTN_SCALE = (MLA_NOPE + MLA_ROPE) ** -0.5
N_EXPERTS = 64
TOP_K = 8
N_GROUPS = 8
TOPK_GROUPS = 4
EXPERT_FF = 256
ROUTED_SCALE = 2.5
COL_KV = MLA_Q_RANK
COL_KR = COL_KV + MLA_KV_RANK
COL_LRU_X = COL_KR + MLA_ROPE
COL_LRU_G = COL_LRU_X + LRU_WIDTH

LANES = 128
SUBLANES = 8
VMEM_LIMIT = 56 * 1024 * 1024

TOKEN_TILE = 512
Q_TILE = 512
KEY_CHUNK = 512
SCAN_CHUNK = 128
SCAN_GROUP = 4
EXPERT_TILE = 256
TILE_STRIDE = EXPERT_TILE + 1
SCATTER_UNROLL = 8
WEIGHT_SPLIT = 4

F32 = jnp.float32
BF16 = jnp.bfloat16
NEG_INF = float("-inf")


def _params(*sem):
    return pltpu.CompilerParams(dimension_semantics=sem, vmem_limit_bytes=VMEM_LIMIT)


def _rms(t, g):
    return t * lax.rsqrt(jnp.mean(t * t, axis=-1, keepdims=True) + EPS) * g


def _gelu_tanh(t):
    return 0.5 * t * (1.0 + jnp.tanh(0.7978845608028654 * (t + 0.044715 * (t * t * t))))


def _silu(t):
    return t * jax.nn.sigmoid(t)


def _rows_to_slabs(rows, stage_ref, slab_ref):
    n = rows.shape[0]
    st = n + 1
    for j in range(SUBLANES):
        stage_ref[j * st:j * st + n, :] = rows[:, j * LANES:(j + 1) * LANES]
    for r in range(n):
        slab_ref[r * SUBLANES:(r + 1) * SUBLANES, :] = stage_ref[pl.ds(r, SUBLANES, stride=st), :]


def _slabs_to_rows(slab_ref, stage_ref, n):
    st = n + 1
    for r in range(n):
        stage_ref[pl.ds(r, SUBLANES, stride=st), :] = slab_ref[r * SUBLANES:(r + 1) * SUBLANES, :]
    return jnp.concatenate([stage_ref[j * st:j * st + n, :] for j in range(SUBLANES)], axis=-1)


def _nt_dot(a, b):
    return lax.dot_general(a, b, (((1,), (1,)), ((), ())), preferred_element_type=F32)


def _mod_kernel(c_ref, w_ref, b_ref, o_ref):
    s = _silu(c_ref[...])
    o_ref[...] = jnp.dot(s, w_ref[...], precision=lax.Precision.HIGHEST,
                         preferred_element_type=F32) + b_ref[...]


def _modulation(cc, w, b):
    rows, d = cc.shape
    n = w.shape[1]
    tn = 1024
    return pl.pallas_call(
        _mod_kernel,
        out_shape=jax.ShapeDtypeStruct((rows, n), F32),
        grid=(n // tn,),
        in_specs=[pl.BlockSpec((rows, d), lambda j: (0, 0)),
                  pl.BlockSpec((d, tn), lambda j: (0, j)),
                  pl.BlockSpec((1, tn), lambda j: (0, j))],
        out_specs=pl.BlockSpec((rows, tn), lambda j: (0, j)),
        compiler_params=_params("arbitrary"),
        name="modulation",
    )(cc, w, b)


def _inproj_kernel(*refs, with_q):
    (x_ref, sh_ref, sc_ref, g_ref, win_ref, qg_ref, kvg_ref, wq_ref, wqs_ref, wk_ref, wks_ref,
     wv_ref, vo_ref, cq_ref, sq_ref, ck_ref, sk_ref) = refs[:17]
    if with_q:
        q_ref, k_ref, v_ref, lx_ref, lg_ref = refs[17:]
    else:
        k_ref, v_ref, lx_ref, lg_ref = refs[17:]
    h = _rms(x_ref[0], g_ref[...]) * (1.0 + sc_ref[0]) + sh_ref[0]
    p = jnp.dot(h.astype(BF16), win_ref[...], preferred_element_type=F32)

    if with_q:
        qn = _rms(p[:, :MLA_Q_RANK], qg_ref[...]).astype(BF16)
        qa = jnp.dot(qn, wq_ref[...], preferred_element_type=F32)
        qb = jnp.dot(qn, wqs_ref[...], preferred_element_type=F32)
        cq, sq = cq_ref[...], sq_ref[...]
        for hd in range(MLA_HEADS):
            sl = slice(hd * LANES, (hd + 1) * LANES)
            q_ref[0, :, sl] = (qa[:, sl] * cq + qb[:, sl] * sq).astype(BF16)

    kvn = _rms(p[:, COL_KV:COL_KR], kvg_ref[...])
    kvn16 = kvn.astype(BF16)
    kin = jnp.concatenate([kvn16, p[:, COL_KR:COL_KR + LANES].astype(BF16)], axis=-1)
    ka = jnp.dot(kin, wk_ref[...], preferred_element_type=F32)
    kb = jnp.dot(kin, wks_ref[...], preferred_element_type=F32)
    ck, sk = ck_ref[...], sk_ref[...]
    for hd in range(MLA_HEADS):
        sl = slice(hd * LANES, (hd + 1) * LANES)
        k_ref[0, :, sl] = (ka[:, sl] * ck + kb[:, sl] * sk).astype(BF16)
    v_ref[0] = (jnp.dot(kvn16, wv_ref[...], preferred_element_type=F32) + vo_ref[...]).astype(BF16)
    lx_ref[0] = p[:, 512:512 + LRU_WIDTH]
    lg_ref[0] = _gelu_tanh(p[:, 1024:1024 + LRU_WIDTH]).astype(BF16)


def _input_projection(x, sh, sc, g, wts, tabs, with_q):
    b_, s_, d = x.shape
    tm = min(TOKEN_TILE, s_)
    hw = MLA_HEADS * LANES
    per_batch = sh.shape[0] == b_
    mod_spec = pl.BlockSpec((1, 1, d), (lambda b, i: (b, 0, 0)) if per_batch else (lambda b, i: (0, 0, 0)))

    def const(a):
        return pl.BlockSpec(a.shape, lambda b, i: (0,) * a.ndim)

    tab_spec = pl.BlockSpec((tm, LANES), lambda b, i: (i, 0))
    wide = lambda w: pl.BlockSpec((1, tm, w), lambda b, i: (b, i, 0))
    weights = (wts["w_in"], wts["q_g"], wts["kv_g"], wts["wq"], wts["wq_sw"], wts["wk"], wts["wk_sw"], wts["wv"],
               wts["v_ones"])
    out_shape = [jax.ShapeDtypeStruct((b_, s_, hw), BF16),
                 jax.ShapeDtypeStruct((b_, s_, hw), BF16),
                 jax.ShapeDtypeStruct((b_, s_, LRU_WIDTH), F32),
                 jax.ShapeDtypeStruct((b_, s_, LRU_WIDTH), BF16)]
    out_specs = [wide(hw), wide(hw), wide(LRU_WIDTH), wide(LRU_WIDTH)]
    if with_q:
        out_shape = [jax.ShapeDtypeStruct((b_, s_, hw), BF16)] + out_shape
        out_specs = [wide(hw)] + out_specs
    return pl.pallas_call(
        functools.partial(_inproj_kernel, with_q=with_q),
        out_shape=out_shape,
        grid=(b_, s_ // tm),
        in_specs=[wide(d), mod_spec, mod_spec, const(g)] + [const(w) for w in weights] + [tab_spec] * 4,
        out_specs=out_specs,
        compiler_params=_params("parallel", "parallel"),
        name="input_projection_lat" if with_q else "input_projection_ctx",
    )(x, sh, sc, g, *weights, *tabs)


def _attn_kernel(q_ref, kl_ref, kc_ref, vl_ref, vc_ref, o_ref):
    s_ = kl_ref.shape[1]
    kc = min(KEY_CHUNK, s_)
    half = LANES // 2
    lane = lax.broadcasted_iota(jnp.int32, (q_ref.shape[1], LANES), 1)
    chunks = [(kl_ref, vl_ref, c * kc, kc) for c in range(s_ // kc)] + [(kc_ref, vc_ref, 0, kc_ref.shape[1])]
    items = [(a, ch) for a in range(2) for ch in chunks]

    def scores(item):
        a, (k_ref, _, start, size) = item
        sl = slice(a * LANES, (a + 1) * LANES)
        return _nt_dot(q_ref[0, :, sl], k_ref[0, start:start + size, sl])

    out = None
    s_next = scores(items[0])
    for i, (a, (_, v_ref, start, size)) in enumerate(items):
        s = s_next
        if i + 1 < len(items):
            s_next = scores(items[i + 1])
        first = i % len(chunks) == 0
        m_c = jnp.max(s, axis=-1, keepdims=True)
        m_new = m_c if first else jnp.maximum(m, m_c)
        p = jnp.exp(s - m_new).astype(BF16)
        pv = jnp.dot(p, v_ref[0, start:start + size, a * LANES:(a + 1) * LANES], preferred_element_type=F32)
        acc = pv if first else acc * jnp.exp(m - m_new) + pv
        m = m_new
        if (i + 1) % len(chunks) == 0:
            own = (lane < half) if a == 0 else (lane >= half)
            row_sum = pltpu.roll(acc, half, 1)
            o = jnp.where(own, acc / row_sum, 0.0)
            out = o if out is None else out + o
    o_ref[0] = out.astype(BF16)


def _attention(q, k_lat, k_ctx, v_lat, v_ctx):
    b_, s_, hw = q.shape
    nc = k_ctx.shape[1]
    tq = min(Q_TILE, s_)
    pair = 2 * LANES
    n_pairs = hw // pair
    return pl.pallas_call(
        _attn_kernel,
        out_shape=jax.ShapeDtypeStruct((b_, s_, n_pairs * LANES), BF16),
        grid=(b_, n_pairs, s_ // tq),
        in_specs=[pl.BlockSpec((1, tq, pair), lambda b, j, i: (b, i, j)),
                  pl.BlockSpec((1, s_, pair), lambda b, j, i: (b, 0, j)),
                  pl.BlockSpec((1, nc, pair), lambda b, j, i: (b, 0, j)),
                  pl.BlockSpec((1, s_, pair), lambda b, j, i: (b, 0, j)),
                  pl.BlockSpec((1, nc, pair), lambda b, j, i: (b, 0, j))],
        out_specs=pl.BlockSpec((1, tq, LANES), lambda b, j, i: (b, i, j)),
        compiler_params=_params("parallel", "parallel", "arbitrary"),
        name="attention",
    )(q, k_lat, k_ctx, v_lat, v_ctx)


def _scan_chunk(a, b, reverse):
    tc = a.shape[0]
    row = lax.broadcasted_iota(jnp.int32, a.shape, 0)
    s = 1
    while s < tc:
        if s < SUBLANES:
            if reverse:
                keep = row < tc - s
                a_sh = jnp.where(keep, pltpu.roll(a, tc - s, 0), 1.0)
                b_sh = jnp.where(keep, pltpu.roll(b, tc - s, 0), 0.0)
            else:
                keep = row >= s
                a_sh = jnp.where(keep, pltpu.roll(a, s, 0), 1.0)
                b_sh = jnp.where(keep, pltpu.roll(b, s, 0), 0.0)
        else:
            ones = jnp.ones((s, a.shape[1]), F32)
            zeros = jnp.zeros((s, a.shape[1]), F32)
            if reverse:
                a_sh = jnp.concatenate([a[s:], ones], axis=0)
                b_sh = jnp.concatenate([b[s:], zeros], axis=0)
            else:
                a_sh = jnp.concatenate([ones, a[:tc - s]], axis=0)
                b_sh = jnp.concatenate([zeros, b[:tc - s]], axis=0)
        b = a * b_sh + b
        a = a * a_sh
        s *= 2
    return a, b


def _lru_kernel(xl_ref, xc_ref, gl_ref, cw_ref, cb_ref, wg_ref, bg_ref, lam_ref, o_ref,
                padl_ref, padc_ref, hf_ref, hb_ref):
    s_ = xl_ref.shape[1]
    nc = xc_ref.shape[1]
    tc = SCAN_CHUNK
    halo = SUBLANES
    zero_halo = jnp.zeros((halo, LANES), F32)
    padl_ref[0:halo, :] = zero_halo
    padl_ref[halo:halo + s_, :] = xl_ref[0]
    padl_ref[halo + s_:2 * halo + s_, :] = zero_halo
    padc_ref[0:halo, :] = zero_halo
    padc_ref[halo:halo + nc, :] = xc_ref[0]
    padc_ref[halo + nc:2 * halo + nc, :] = zero_halo

    cw = cw_ref[...]
    cb = cb_ref[...]

    def coeffs(pad_ref, j, d):
        win = pad_ref[pl.ds(pl.multiple_of(j * tc, SUBLANES), tc + 2 * halo), :]
        n = tc + 2 * halo
        u = (cb + cw[0:1] * pltpu.roll(win, 2, 0)[halo:halo + tc]
             + cw[1:2] * pltpu.roll(win, 1, 0)[halo:halo + tc]
             + cw[2:3] * win[halo:halo + tc]
             + cw[3:4] * pltpu.roll(win, n - 1, 0)[halo:halo + tc])
        gates = jnp.dot(u.astype(BF16), wg_ref[d, 0], preferred_element_type=F32) + bg_ref[d, 0]
        r = jax.nn.sigmoid(gates[:, :LANES])
        i = jax.nn.sigmoid(gates[:, LANES:])
        z = -lam_ref[d]
        softplus = jnp.maximum(z, 0.0) + jnp.log(1.0 + jnp.exp(-jnp.abs(z)))
        log_a = (-RG_C) * r * softplus
        a = jnp.exp(log_a)
        b = jnp.sqrt(1.0 - a * a) * (i * u)
        return a, b

    def local_scan(pad_ref, j, d):
        a, b = coeffs(pad_ref, j, d)
        return _scan_chunk(a, b, reverse=(d == 1))

    def apply_carry(scanned, d, carry):
        a_cum, b_loc = scanned
        h = a_cum * carry + b_loc
        return h, (h[0:1] if d == 1 else h[tc - 1:tc])

    cf = cb_ = jnp.zeros((1, LANES), F32)
    n_ctx = nc // tc
    for j in range(n_ctx):
        _, cf = apply_carry(local_scan(padc_ref, j, 0), 0, cf)
        _, cb_ = apply_carry(local_scan(padc_ref, n_ctx - 1 - j, 1), 1, cb_)

    n_lat = s_ // tc
    grp = min(SCAN_GROUP, n_lat)

    def body(jj, carry):
        cf, cb_ = carry
        fwd = [jj * grp + g for g in range(grp)]
        bwd = [n_lat - 1 - jj * grp - g for g in range(grp)]
        scans_f = [local_scan(padl_ref, j, 0) for j in fwd]
        scans_b = [local_scan(padl_ref, j, 1) for j in bwd]
        for j, sc in zip(fwd, scans_f):
            h, cf = apply_carry(sc, 0, cf)
            hf_ref[pl.ds(pl.multiple_of(j * tc, tc), tc), :] = h
        for j, sc in zip(bwd, scans_b):
            h, cb_ = apply_carry(sc, 1, cb_)
            hb_ref[pl.ds(pl.multiple_of(j * tc, tc), tc), :] = h
        return cf, cb_

    lax.fori_loop(0, n_lat // grp, body, (cf, cb_))
    o_ref[0] = ((hf_ref[...] + hb_ref[...]) * gl_ref[0].astype(F32)).astype(BF16)


def _rglru(lx_lat, lx_ctx, gl, conv_w, conv_b, wg, bg, lam):
    b_, s_, w = lx_lat.shape
    nc = lx_ctx.shape[1]
    ng = w // LANES
    halo = SUBLANES
    return pl.pallas_call(
        _lru_kernel,
        out_shape=jax.ShapeDtypeStruct((b_, s_, w), BF16),
        grid=(b_, ng),
        in_specs=[pl.BlockSpec((1, s_, LANES), lambda b, g: (b, 0, g)),
                  pl.BlockSpec((1, nc, LANES), lambda b, g: (b, 0, g)),
                  pl.BlockSpec((1, s_, LANES), lambda b, g: (b, 0, g)),
                  pl.BlockSpec((CONV_W, LANES), lambda b, g: (0, g)),
                  pl.BlockSpec((1, LANES), lambda b, g: (0, g)),
                  pl.BlockSpec((2, 1, LANES, 2 * LANES), lambda b, g: (0, g, 0, 0)),
                  pl.BlockSpec((2, 1, 1, 2 * LANES), lambda b, g: (0, g, 0, 0)),
                  pl.BlockSpec((2, 1, LANES), lambda b, g: (0, 0, g))],
        out_specs=pl.BlockSpec((1, s_, LANES), lambda b, g: (b, 0, g)),
        scratch_shapes=[pltpu.VMEM((s_ + 2 * halo, LANES), F32),
                        pltpu.VMEM((nc + 2 * halo, LANES), F32),
                        pltpu.VMEM((s_, LANES), F32),
                        pltpu.VMEM((s_, LANES), F32)],
        compiler_params=_params("parallel", "parallel"),
        name="rglru",
    )(lx_lat, lx_ctx, gl, conv_w, conv_b, wg, bg, lam)


def _first_index(mask, iota, limit, axis):
    return jnp.min(jnp.where(mask, iota, limit), axis=axis, keepdims=True)


def _outproj_kernel(o_ref, y2_ref, x_ref, g1_ref, sh_ref, sc_ref, g2_ref, gf_ref, wo_ref, wrt_ref, rb_ref,
                    wsgu_ref, wsd_ref, xs_ref, fp_ref, e_ref, w_ref, stage_ref):
    half = wo_ref.shape[0] // 2
    mix = (jnp.dot(o_ref[0], wo_ref[0:half, :], preferred_element_type=F32)
           + jnp.dot(y2_ref[0], wo_ref[half:, :], preferred_element_type=F32))
    x1 = x_ref[0] + g1_ref[0] * mix
    f = _rms(x1, gf_ref[...]) * (1.0 + sc_ref[0]) + sh_ref[0]
    f16 = f.astype(BF16)
    tm = f.shape[0]

    logits = lax.dot_general(wrt_ref[...], f, (((1,), (1,)), ((), ())),
                             precision=lax.Precision.HIGHEST, preferred_element_type=F32)
    scores = jax.nn.sigmoid(logits)
    sel = scores + rb_ref[...]
    per = N_EXPERTS // N_GROUPS
    g3 = sel.reshape(N_GROUPS, per, tm)
    mem = lax.broadcasted_iota(jnp.int32, g3.shape, 1)
    m1 = jnp.max(g3, axis=1, keepdims=True)
    first = _first_index(g3 == m1, mem, per, 1)
    m2 = jnp.max(jnp.where(mem == first, NEG_INF, g3), axis=1, keepdims=True)
    gscore = (m1 + m2).reshape(N_GROUPS, tm)
    giota = lax.broadcasted_iota(jnp.int32, gscore.shape, 0)
    gmask = jnp.zeros(gscore.shape, F32)
    cur = gscore
    for _ in range(TOPK_GROUPS):
        mx = jnp.max(cur, axis=0, keepdims=True)
        pick = giota == _first_index(cur == mx, giota, N_GROUPS, 0)
        gmask = jnp.where(pick, 1.0, gmask)
        cur = jnp.where(pick, NEG_INF, cur)
    allowed = jnp.broadcast_to(gmask.reshape(N_GROUPS, 1, tm), g3.shape) > 0.0
    cur = jnp.where(allowed, g3, NEG_INF).reshape(N_EXPERTS, tm)
    eiota = lax.broadcasted_iota(jnp.int32, cur.shape, 0)
    picked_w = []
    for k in range(TOP_K):
        mx = jnp.max(cur, axis=0, keepdims=True)
        idx = _first_index(cur == mx, eiota, N_EXPERTS, 0)
        pick = eiota == idx
        e_ref[k:k + 1, :] = idx
        picked_w.append(jnp.sum(jnp.where(pick, scores, 0.0), axis=0, keepdims=True))
        cur = jnp.where(pick, NEG_INF, cur)
    wsum = picked_w[0]
    for k in range(1, TOP_K):
        wsum = wsum + picked_w[k]
    for k in range(TOP_K):
        w_ref[k:k + 1, :] = ROUTED_SCALE * picked_w[k] / wsum

    a = jnp.dot(f16, wsgu_ref[...], preferred_element_type=F32)
    ff = a.shape[1] // 2
    act = _silu(a[:, :ff]) * a[:, ff:]
    shared = jnp.dot(act.astype(BF16), wsd_ref[...], preferred_element_type=F32)
    xs_ref[0] = x1 + g2_ref[0] * shared
    _rows_to_slabs(f, stage_ref, fp_ref.at[0])


def _output_projection(o, y2, x, g1, sh2, sc2, g2, gf, wts):
    b_, s_, d = x.shape
    tm = min(TOKEN_TILE, s_)
    nt = s_ // tm
    mod_spec = pl.BlockSpec((1, 1, d), lambda b, i: (b, 0, 0))

    def const(a):
        return pl.BlockSpec(a.shape, lambda b, i: (0,) * a.ndim)

    wide = lambda w: pl.BlockSpec((1, tm, w), lambda b, i: (b, i, 0))
    route_spec = pl.BlockSpec((TOP_K, tm), lambda b, i: (0, b * nt + i))
    weights = (wts["w_out"], wts["router_t"], wts["router_b"], wts["sh_gu"], wts["sh_d"])
    return pl.pallas_call(
        _outproj_kernel,
        out_shape=[jax.ShapeDtypeStruct((b_, s_, d), F32),
                   jax.ShapeDtypeStruct((b_, s_ * SUBLANES, LANES), F32),
                   jax.ShapeDtypeStruct((TOP_K, b_ * s_), jnp.int32),
                   jax.ShapeDtypeStruct((TOP_K, b_ * s_), F32)],
        grid=(b_, nt),
        in_specs=[wide(o.shape[2]), wide(y2.shape[2]), wide(d), mod_spec, mod_spec, mod_spec, mod_spec,
                  const(gf)] + [const(w) for w in weights],
        out_specs=[wide(d), pl.BlockSpec((1, tm * SUBLANES, LANES), lambda b, i: (b, i, 0)), route_spec, route_spec],
        scratch_shapes=[pltpu.VMEM((SUBLANES * (tm + 1), LANES), F32)],
        compiler_params=_params("parallel", "parallel"),
        name="output_projection",
    )(o, y2, x, g1, sh2, sc2, g2, gf, *weights)


def _moe_kernel(tile_e_ref, tile_src_ref, tile_n_ref, ntiles_ref, tok_ref, wl_ref, fp_ref, *refs):
    wgu_refs = refs[:WEIGHT_SPLIT]
    wd_refs = refs[WEIGHT_SPLIT:2 * WEIGHT_SPLIT]
    acc_ref, tin_a, tin_b, tout_a, tout_b = refs[2 * WEIGHT_SPLIT:]
    t = pl.program_id(0)
    m = EXPERT_TILE
    st = TILE_STRIDE
    n_out = tout_a.shape[0] // st

    def slab(off):
        return pl.ds(pl.multiple_of(off, SUBLANES), SUBLANES)

    def gather(tile, tin):
        base = tile_src_ref[tile]
        for mi in range(m):
            tin[pl.ds(mi, SUBLANES, stride=st), :] = fp_ref[slab(tok_ref[base + mi]), :]

    def experts(tile, tin, tout):
        xt = jnp.concatenate([tin[j * st:j * st + m, :].astype(BF16) for j in range(n_out)], axis=-1)
        hcat = _chunked_dot(xt, wgu_refs)
        ff = hcat.shape[1] // 2
        act = _silu(hcat[:, :ff]) * hcat[:, ff:]
        y = _chunked_dot(act.astype(BF16), wd_refs)
        valid = lax.broadcasted_iota(jnp.int32, (m, LANES), 0) < tile_n_ref[tile]
        for j in range(n_out):
            tout[j * st:j * st + m, :] = jnp.where(valid, y[:, j * LANES:(j + 1) * LANES], 0.0)

    def scatter(tile, tout):
        base = tile_src_ref[tile]
        for g in range(m // SCATTER_UNROLL):
            pend = []
            for r in range(SCATTER_UNROLL):
                row = g * SCATTER_UNROLL + r
                dst = slab(tok_ref[base + row])
                pend.append((dst, acc_ref[dst, :] + wl_ref[base + row] * tout[pl.ds(row, SUBLANES, stride=st), :]))
            for dst, new in reversed(pend):
                acc_ref[dst, :] = new

    @pl.when(t == 0)
    def _():
        acc_ref[...] = jnp.zeros(acc_ref.shape, F32)
        tout_b[...] = jnp.zeros(tout_b.shape, F32)
        gather(0, tin_a)

    def step(tin_cur, tin_nxt, tout_cur, tout_prev):
        gather(t + 1, tin_nxt)
        experts(t, tin_cur, tout_cur)
        scatter(jnp.maximum(t - 1, 0), tout_prev)

    active = t <= ntiles_ref[0]

    @pl.when(active & (t % 2 == 0))
    def _():
        step(tin_a, tin_b, tout_a, tout_b)

    @pl.when(active & (t % 2 == 1))
    def _():
        step(tin_b, tin_a, tout_b, tout_a)


def _chunked_dot(lhs, w_refs):
    kc = w_refs[0].shape[1]
    out = None
    for c, w_ref in enumerate(w_refs):
        part = jnp.dot(lhs[:, c * kc:(c + 1) * kc], w_ref[0], preferred_element_type=F32)
        out = part if out is None else out + part
    return out


def _routed_experts(tables, tok, wl, buf, wgu, wd, b):
    tile_e, tile_src, tile_n, ntiles = tables
    d = wd.shape[2]
    _, rows, _ = buf.shape
    assert d == SUBLANES * LANES
    stage = d // LANES * TILE_STRIDE
    n_steps = tile_e.shape[0] - 1
    block = pl.BlockSpec((None, rows, LANES), lambda t, *_: (b, 0, 0), pipeline_mode=pl.Buffered(1))

    def split(w):
        kc = w.shape[1] // WEIGHT_SPLIT
        return [pl.BlockSpec((1, kc, w.shape[2]), lambda t, te, *_, c=c: (te[t], c, 0)) for c in range(WEIGHT_SPLIT)]

    grid_spec = pltpu.PrefetchScalarGridSpec(
        num_scalar_prefetch=6,
        grid=(n_steps,),
        in_specs=[block] + split(wgu) + split(wd),
        out_specs=block,
        scratch_shapes=[pltpu.VMEM((stage, LANES), F32) for _ in range(4)],
    )
    return pl.pallas_call(
        _moe_kernel,
        out_shape=jax.ShapeDtypeStruct(buf.shape, F32),
        grid_spec=grid_spec,
        input_output_aliases={6: 0},
        compiler_params=_params("arbitrary"),
        name="routed_experts",
    )(tile_e, tile_src, tile_n, ntiles, tok, wl, buf, *([wgu] * WEIGHT_SPLIT), *([wd] * WEIGHT_SPLIT))


def _final_kernel(xs_ref, r_ref, g2_ref, g_ref, o_ref, stage_ref):
    routed = _slabs_to_rows(r_ref, stage_ref, xs_ref.shape[1])
    o_ref[0] = _rms(xs_ref[0] + g2_ref[0] * routed, g_ref[...])


def _final(xs, routed, g2, g):
    b_, s_, d = xs.shape
    tm = min(TOKEN_TILE, s_)
    nt = s_ // tm
    wide = pl.BlockSpec((1, tm, d), lambda b, i: (b, i, 0))
    return pl.pallas_call(
        _final_kernel,
        out_shape=jax.ShapeDtypeStruct((b_, s_, d), F32),
        grid=(b_, nt),
        in_specs=[wide, pl.BlockSpec((tm * SUBLANES, LANES), lambda b, i: (b * nt + i, 0)),
                  pl.BlockSpec((1, 1, d), lambda b, i: (b, 0, 0)),
                  pl.BlockSpec((1, d), lambda b, i: (0, 0))],
        out_specs=wide,
        scratch_shapes=[pltpu.VMEM((SUBLANES * (tm + 1), LANES), F32)],
        compiler_params=_params("parallel", "parallel"),
        name="final_norm",
    )(xs, routed, g2, g)


def _prep_weights(w_in, q_norm_g, w_q_up, kv_norm_g, w_kv_up, lru_w_a, lru_b_a, lru_w_x, lru_b_x, w_out,
                  router_w, router_bias, exp_w_gate, exp_w_up, exp_w_down, sh_w_gate, sh_w_up, sh_w_down):
    d = w_in.shape[0]
    h_ = MLA_HEADS
    pad_kr = jnp.zeros((d, LANES - MLA_ROPE), F32)
    w_in_p = jnp.concatenate([w_in[:, :COL_LRU_X], pad_kr, w_in[:, COL_LRU_X:]], axis=1).astype(BF16)

    wq = w_q_up.reshape(MLA_Q_RANK, h_, MLA_NOPE + MLA_ROPE)
    nope, rope = wq[:, :, :MLA_NOPE], wq[:, :, MLA_NOPE:]
    rope_sw = rope.reshape(MLA_Q_RANK, h_, 2, 2, ROPE_AXIS // 2)[:, :, :, ::-1, :].reshape(rope.shape)
    zpad = jnp.zeros((MLA_Q_RANK, h_, LANES - MLA_NOPE - MLA_ROPE), F32)
    wq_p = jnp.concatenate([nope, rope, zpad], axis=-1).reshape(MLA_Q_RANK, h_ * LANES).astype(BF16)
    wq_sw = jnp.concatenate([jnp.zeros_like(nope), rope_sw, zpad], axis=-1).reshape(MLA_Q_RANK, h_ * LANES).astype(BF16)

    wkv = w_kv_up.reshape(MLA_KV_RANK, h_, MLA_NOPE + MLA_V)
    k_nope, v_w = wkv[:, :, :MLA_NOPE], wkv[:, :, MLA_NOPE:]
    r_idx = jnp.arange(MLA_ROPE)
    place = jnp.zeros((LANES, h_, LANES), F32).at[r_idx, :, MLA_NOPE + r_idx].set(1.0)
    place_sw = jnp.zeros((LANES, h_, LANES), F32).at[r_idx ^ (ROPE_AXIS // 2), :, MLA_NOPE + r_idx].set(1.0)
    k_top = jnp.concatenate([k_nope, jnp.zeros((MLA_KV_RANK, h_, LANES - MLA_NOPE), F32)], axis=-1)
    wk = jnp.concatenate([k_top, place], axis=0).reshape(MLA_KV_RANK + LANES, h_ * LANES).astype(BF16)
    wk_sw = jnp.concatenate([jnp.zeros_like(k_top), place_sw], axis=0).reshape(MLA_KV_RANK + LANES, h_ * LANES).astype(BF16)
    zv = jnp.zeros_like(v_w)
    even = (jnp.arange(h_) % 2 == 0)[None, :, None]
    wv = jnp.concatenate([jnp.where(even, v_w, zv), jnp.where(even, zv, v_w)], axis=-1)
    wv = wv.reshape(MLA_KV_RANK, h_ * LANES).astype(BF16)
    one_lo = jnp.concatenate([jnp.zeros((MLA_V,), F32), jnp.ones((LANES - MLA_V,), F32)])
    one_hi = jnp.concatenate([jnp.ones((LANES - MLA_V,), F32), jnp.zeros((MLA_V,), F32)])
    v_ones = jnp.where(even[0], one_lo[None, :], one_hi[None, :]).reshape(1, h_ * LANES)

    eye = jnp.eye(LRU_BLOCKS, dtype=F32)
    ng = LRU_WIDTH // LANES

    def dense(w):
        return jnp.einsum("xncd,nm->xncmd", w, eye).reshape(2, LRU_WIDTH, LRU_WIDTH)

    def grp(wd_):
        return jnp.stack([wd_[:, g * LANES:(g + 1) * LANES, g * LANES:(g + 1) * LANES] for g in range(ng)], axis=1)

    wg = jnp.concatenate([grp(dense(lru_w_a)), grp(dense(lru_w_x))], axis=-1).astype(BF16)
    bg = jnp.concatenate([lru_b_a.reshape(2, ng, 1, LANES), lru_b_x.reshape(2, ng, 1, LANES)], axis=-1)

    return dict(
        w_in=w_in_p, q_g=q_norm_g[None], kv_g=kv_norm_g[None], wq=wq_p, wq_sw=wq_sw, wk=wk, wk_sw=wk_sw, wv=wv,
        v_ones=v_ones,
        wg=wg, bg=bg,
        w_out=w_out.astype(BF16), router_t=router_w.T, router_b=router_bias[:, None],
        sh_gu=jnp.concatenate([sh_w_gate, sh_w_up], axis=1).astype(BF16), sh_d=sh_w_down.astype(BF16),
        exp_gu=jnp.concatenate([exp_w_gate, exp_w_up], axis=2).astype(BF16), exp_d=exp_w_down.astype(BF16),
    )


def _rope_tables(s_, nc):
    rows = s_ // GRID_W
    row = jnp.repeat(jnp.arange(rows, dtype=F32), GRID_W)
    col = jnp.tile(jnp.arange(GRID_W, dtype=F32), rows)
    inv_freq = ROPE_BASE ** (-jnp.arange(0, ROPE_AXIS, 2, dtype=F32) / ROPE_AXIS)
    ang = jnp.stack([row, col], axis=-1)[:, :, None] * inv_freq
    cos = jnp.broadcast_to(jnp.cos(ang)[:, :, None, :], (s_, 2, 2, ROPE_AXIS // 2)).reshape(s_, MLA_ROPE)
    sin = jnp.sin(ang)[:, :, None, :] * jnp.array([-1.0, 1.0], F32)[None, None, :, None]
    sin = sin.reshape(s_, MLA_ROPE)
    ones = jnp.ones((s_, MLA_NOPE), F32)
    zeros = jnp.zeros((s_, LANES - MLA_NOPE - MLA_ROPE), F32)
    c_tab = jnp.concatenate([ones, cos, zeros], axis=1)
    s_tab = jnp.concatenate([jnp.zeros_like(ones), sin, zeros], axis=1)
    c_ctx = jnp.concatenate([jnp.ones((nc, MLA_NOPE + MLA_ROPE), F32),
                             jnp.zeros((nc, LANES - MLA_NOPE - MLA_ROPE), F32)], axis=1)
    return c_tab, s_tab, c_ctx, jnp.zeros_like(c_ctx)


def _dispatch(eidx, wts, b_, s_):
    n = TOP_K * s_
    e = eidx.reshape(TOP_K, b_, s_).transpose(1, 0, 2).reshape(b_, n)
    w = wts.reshape(TOP_K, b_, s_).transpose(1, 0, 2).reshape(b_, n)
    key = e * n + jnp.arange(n, dtype=jnp.int32)[None, :]
    key_sorted, w_sorted = lax.sort((key, w), dimension=1, num_keys=1)
    tok = ((key_sorted % n) % s_) * SUBLANES
    tok = jnp.pad(tok.astype(jnp.int32), ((0, 0), (0, EXPERT_TILE)))
    w_sorted = jnp.pad(w_sorted, ((0, 0), (0, EXPERT_TILE)))

    ids = jnp.arange(N_EXPERTS, dtype=jnp.int32)
    counts = jnp.sum(e[:, :, None] == ids[None, None, :], axis=1, dtype=jnp.int32)
    starts = jnp.cumsum(counts, axis=1) - counts
    tiles = (counts + EXPERT_TILE - 1) // EXPERT_TILE
    tile_end = jnp.cumsum(tiles, axis=1)
    ntiles = tile_end[:, -1:]
    tid = jnp.arange(n // EXPERT_TILE + N_EXPERTS + 2, dtype=jnp.int32)[None, :]
    tile_e = jnp.sum(tid[:, :, None] >= tile_end[:, None, :], axis=-1, dtype=jnp.int32)
    tile_e = jnp.minimum(tile_e, N_EXPERTS - 1)
    take = lambda a: jnp.take_along_axis(a, tile_e, axis=1)
    within = (tid - (take(tile_end) - take(tiles))) * EXPERT_TILE
    live = tid < ntiles
    tile_src = jnp.where(live, take(starts) + within, 0)
    tile_n = jnp.where(live, jnp.clip(take(counts) - within, 0, EXPERT_TILE), 0)
    return (tile_e, tile_src, tile_n, ntiles), tok, w_sorted


def kernel(x, c, ctx, c_ctx, w_mod, b_mod, norm_mix_g, w_in, q_norm_g, w_q_up, kv_norm_g, w_kv_up, conv_w, conv_b,
           lru_w_a, lru_b_a, lru_w_x, lru_b_x, lru_lambda, w_out, norm_ffn_g, router_w, router_bias, exp_w_gate,
           exp_w_up, exp_w_down, sh_w_gate, sh_w_up, sh_w_down, final_norm_g):
    b_, s_, d = x.shape
    nc = ctx.shape[1]
    assert w_mod.shape[0] == 1, "single-layer operation"
    assert s_ % GRID_W == 0 and s_ % SCAN_CHUNK == 0 and nc % SCAN_CHUNK == 0
    assert s_ % min(TOKEN_TILE, s_) == 0 and s_ % min(Q_TILE, s_) == 0

    wts = _prep_weights(w_in[0], q_norm_g[0], w_q_up[0], kv_norm_g[0], w_kv_up[0], lru_w_a[0], lru_b_a[0],
                        lru_w_x[0], lru_b_x[0], w_out[0], router_w[0], router_bias[0], exp_w_gate[0], exp_w_up[0],
                        exp_w_down[0], sh_w_gate[0], sh_w_up[0], sh_w_down[0])
    c_tab, s_tab, c_ctx_tab, s_ctx_tab = _rope_tables(s_, nc)

    rows = (b_ + 1 + SUBLANES - 1) // SUBLANES * SUBLANES
    cc = jnp.zeros((rows, d), F32).at[:b_].set(c).at[b_].set(c_ctx)
    mod = _modulation(cc, w_mod[0], b_mod[0][None])
    sh1, sc1, g1, sh2, sc2, g2 = [mod[:b_, i * d:(i + 1) * d].reshape(b_, 1, d) for i in range(6)]
    csh1, csc1 = [mod[b_, i * d:(i + 1) * d].reshape(1, 1, d) for i in range(2)]

    q, k_lat, v_lat, lx_lat, gl = _input_projection(
        x, sh1, sc1, norm_mix_g, wts, (c_tab * ATTN_SCALE, s_tab * ATTN_SCALE, c_tab, s_tab), True)
    k_ctx, v_ctx, lx_ctx, _ = _input_projection(
        ctx, csh1, csc1, norm_mix_g, wts, (c_ctx_tab, s_ctx_tab, c_ctx_tab, s_ctx_tab), False)

    o_lat = _attention(q, k_lat, k_ctx, v_lat, v_ctx)
    y2 = _rglru(lx_lat, lx_ctx, gl, conv_w[0], conv_b[0][None], wts["wg"], wts["bg"],
                lru_lambda[0].reshape(2, 1, LRU_WIDTH))

    xs, fp, eidx, rw = _output_projection(o_lat, y2, x, g1, sh2, sc2, g2, norm_ffn_g, wts)

    tables, tok, wl = _dispatch(eidx, rw, b_, s_)
    buf = fp
    for b in range(b_):
        buf = _routed_experts([tb[b] for tb in tables], tok[b], wl[b], buf, wts["exp_gu"], wts["exp_d"], b)
    routed = buf.reshape(b_ * s_ * SUBLANES, LANES)

    return _final(xs, routed, g2, final_norm_g[None])
```

```python
import functools

import jax
import jax.numpy as jnp
from jax import lax
from jax.experimental import pallas as pl
from jax.experimental.pallas import tpu as pltpu

GRID_W = 64
EPS = 1e-6
MLA_HEADS = 8
MLA_NOPE = 64
MLA_ROPE = 32
MLA_V = 64
MLA_Q_RANK = 256
MLA_KV_RANK = 128
LRU_WIDTH = 512
LRU_BLOCKS = 8
CONV_W = 4
RG_C = 8.0
ROPE_BASE = 10000.0
ROPE_AXIS = MLA_ROPE // 2
ATTN_SCALE = (MLA_NOPE + MLA_ROPE) ** -0.5
N_EXPERTS = 64
TOP_K = 8
N_GROUPS = 8
TOPK_GROUPS = 4
EXPERT_FF = 256
ROUTED_SCALE = 2.5
COL_KV = MLA_Q_RANK
COL_KR = COL_KV + MLA_KV_RANK
COL_LRU_X = COL_KR + MLA_ROPE
COL_LRU_G = COL_LRU_X + LRU_WIDTH

LANES = 128
SUBLANES = 8
VMEM_LIMIT = 56 * 1024 * 1024

TOKEN_TILE = 512
Q_TILE = 512
KEY_CHUNK = 512
SCAN_CHUNK = 128
SCAN_GROUP = 4
EXPERT_TILE = 256
TILE_STRIDE = EXPERT_TILE + 1
SCATTER_UNROLL = 8

F32 = jnp.float32
BF16 = jnp.bfloat16
NEG_INF = float("-inf")


def _params(*sem):
    return pltpu.CompilerParams(dimension_semantics=sem, vmem_limit_bytes=VMEM_LIMIT)


def _rms(t, g):
    return t * lax.rsqrt(jnp.mean(t * t, axis=-1, keepdims=True) + EPS) * g


def _gelu_tanh(t):
    return 0.5 * t * (1.0 + jnp.tanh(0.7978845608028654 * (t + 0.044715 * (t * t * t))))


def _silu(t):
    return t * jax.nn.sigmoid(t)


def _rows_to_slabs(rows, stage_ref, slab_ref):
    n = rows.shape[0]
    st = n + 1
    for j in range(SUBLANES):
        stage_ref[j * st:j * st + n, :] = rows[:, j * LANES:(j + 1) * LANES]
    for r in range(n):
        slab_ref[r * SUBLANES:(r + 1) * SUBLANES, :] = stage_ref[pl.ds(r, SUBLANES, stride=st), :]


def _slabs_to_rows(slab_ref, stage_ref, n):
    st = n + 1
    for r in range(n):
        stage_ref[pl.ds(r, SUBLANES, stride=st), :] = slab_ref[r * SUBLANES:(r + 1) * SUBLANES, :]
    return jnp.concatenate([stage_ref[j * st:j * st + n, :] for j in range(SUBLANES)], axis=-1)


def _nt_dot(a, b):
    return lax.dot_general(a, b, (((1,), (1,)), ((), ())), preferred_element_type=F32)


def _mod_kernel(c_ref, w_ref, b_ref, o_ref):
    s = _silu(c_ref[...])
    o_ref[...] = jnp.dot(s, w_ref[...], precision=lax.Precision.HIGHEST,
                         preferred_element_type=F32) + b_ref[...]


def _modulation(cc, w, b):
    rows, d = cc.shape
    n = w.shape[1]
    tn = 1024
    return pl.pallas_call(
        _mod_kernel,
        out_shape=jax.ShapeDtypeStruct((rows, n), F32),
        grid=(n // tn,),
        in_specs=[pl.BlockSpec((rows, d), lambda j: (0, 0)),
                  pl.BlockSpec((d, tn), lambda j: (0, j)),
                  pl.BlockSpec((1, tn), lambda j: (0, j))],
        out_specs=pl.BlockSpec((rows, tn), lambda j: (0, j)),
        compiler_params=_params("arbitrary"),
        name="modulation",
    )(cc, w, b)


def _inproj_kernel(*refs, with_q):
    (x_ref, sh_ref, sc_ref, g_ref, win_ref, qg_ref, kvg_ref, wq_ref, wqs_ref, wk_ref, wks_ref,
     wv_ref, vo_ref, cq_ref, sq_ref, ck_ref, sk_ref) = refs[:17]
    if with_q:
        q_ref, k_ref, v_ref, lx_ref, lg_ref = refs[17:]
    else:
        k_ref, v_ref, lx_ref, lg_ref = refs[17:]
    h = _rms(x_ref[0], g_ref[...]) * (1.0 + sc_ref[0]) + sh_ref[0]
    p = jnp.dot(h.astype(BF16), win_ref[...], preferred_element_type=F32)

    if with_q:
        qn = _rms(p[:, :MLA_Q_RANK], qg_ref[...]).astype(BF16)
        qa = jnp.dot(qn, wq_ref[...], preferred_element_type=F32)
        qb = jnp.dot(qn, wqs_ref[...], preferred_element_type=F32)
        cq, sq = cq_ref[...], sq_ref[...]
        for hd in range(MLA_HEADS):
            sl = slice(hd * LANES, (hd + 1) * LANES)
            q_ref[0, :, sl] = (qa[:, sl] * cq + qb[:, sl] * sq).astype(BF16)

    kvn = _rms(p[:, COL_KV:COL_KR], kvg_ref[...])
    kvn16 = kvn.astype(BF16)
    kin = jnp.concatenate([kvn16, p[:, COL_KR:COL_KR + LANES].astype(BF16)], axis=-1)
    ka = jnp.dot(kin, wk_ref[...], preferred_element_type=F32)
    kb = jnp.dot(kin, wks_ref[...], preferred_element_type=F32)
    ck, sk = ck_ref[...], sk_ref[...]
    for hd in range(MLA_HEADS):
        sl = slice(hd * LANES, (hd + 1) * LANES)
        k_ref[0, :, sl] = (ka[:, sl] * ck + kb[:, sl] * sk).astype(BF16)
    v_ref[0] = (jnp.dot(kvn16, wv_ref[...], preferred_element_type=F32) + vo_ref[...]).astype(BF16)
    lx_ref[0] = p[:, 512:512 + LRU_WIDTH]
    lg_ref[0] = _gelu_tanh(p[:, 1024:1024 + LRU_WIDTH]).astype(BF16)


def _input_projection(x, sh, sc, g, wts, tabs, with_q):
    b_, s_, d = x.shape
    tm = min(TOKEN_TILE, s_)
    hw = MLA_HEADS * LANES
    per_batch = sh.shape[0] == b_
    mod_spec = pl.BlockSpec((1, 1, d), (lambda b, i: (b, 0, 0)) if per_batch else (lambda b, i: (0, 0, 0)))

    def const(a):
        return pl.BlockSpec(a.shape, lambda b, i: (0,) * a.ndim)

    tab_spec = pl.BlockSpec((tm, LANES), lambda b, i: (i, 0))
    wide = lambda w: pl.BlockSpec((1, tm, w), lambda b, i: (b, i, 0))
    weights = (wts["w_in"], wts["q_g"], wts["kv_g"], wts["wq"], wts["wq_sw"], wts["wk"], wts["wk_sw"], wts["wv"],
               wts["v_ones"])
    out_shape = [jax.ShapeDtypeStruct((b_, s_, hw), BF16),
                 jax.ShapeDtypeStruct((b_, s_, hw), BF16),
                 jax.ShapeDtypeStruct((b_, s_, LRU_WIDTH), F32),
                 jax.ShapeDtypeStruct((b_, s_, LRU_WIDTH), BF16)]
    out_specs = [wide(hw), wide(hw), wide(LRU_WIDTH), wide(LRU_WIDTH)]
    if with_q:
        out_shape = [jax.ShapeDtypeStruct((b_, s_, hw), BF16)] + out_shape
        out_specs = [wide(hw)] + out_specs
    return pl.pallas_call(
        functools.partial(_inproj_kernel, with_q=with_q),
        out_shape=out_shape,
        grid=(b_, s_ // tm),
        in_specs=[wide(d), mod_spec, mod_spec, const(g)] + [const(w) for w in weights] + [tab_spec] * 4,
        out_specs=out_specs,
        compiler_params=_params("parallel", "parallel"),
        name="input_projection_lat" if with_q else "input_projection_ctx",
    )(x, sh, sc, g, *weights, *tabs)


def _attn_kernel(q_ref, kl_ref, kc_ref, vl_ref, vc_ref, o_ref):
    s_ = kl_ref.shape[1]
    kc = min(KEY_CHUNK, s_)
    half = LANES // 2
    lane = lax.broadcasted_iota(jnp.int32, (q_ref.shape[1], LANES), 1)
    chunks = [(kl_ref, vl_ref, c * kc, kc) for c in range(s_ // kc)] + [(kc_ref, vc_ref, 0, kc_ref.shape[1])]
    items = [(a, ch) for a in range(2) for ch in chunks]

    def scores(item):
        a, (k_ref, _, start, size) = item
        sl = slice(a * LANES, (a + 1) * LANES)
        return _nt_dot(q_ref[0, :, sl], k_ref[0, start:start + size, sl])

    out = None
    s_next = scores(items[0])
    for i, (a, (_, v_ref, start, size)) in enumerate(items):
        s = s_next
        if i + 1 < len(items):
            s_next = scores(items[i + 1])
        first = i % len(chunks) == 0
        m_c = jnp.max(s, axis=-1, keepdims=True)
        m_new = m_c if first else jnp.maximum(m, m_c)
        p = jnp.exp(s - m_new).astype(BF16)
        pv = jnp.dot(p, v_ref[0, start:start + size, a * LANES:(a + 1) * LANES], preferred_element_type=F32)
        acc = pv if first else acc * jnp.exp(m - m_new) + pv
        m = m_new
        if (i + 1) % len(chunks) == 0:
            own = (lane < half) if a == 0 else (lane >= half)
            row_sum = pltpu.roll(acc, half, 1)
            o = jnp.where(own, acc / row_sum, 0.0)
            out = o if out is None else out + o
    o_ref[0] = out.astype(BF16)


def _attention(q, k_lat, k_ctx, v_lat, v_ctx):
    b_, s_, hw = q.shape
    nc = k_ctx.shape[1]
    tq = min(Q_TILE, s_)
    pair = 2 * LANES
    n_pairs = hw // pair
    return pl.pallas_call(
        _attn_kernel,
        out_shape=jax.ShapeDtypeStruct((b_, s_, n_pairs * LANES), BF16),
        grid=(b_, n_pairs, s_ // tq),
        in_specs=[pl.BlockSpec((1, tq, pair), lambda b, j, i: (b, i, j)),
                  pl.BlockSpec((1, s_, pair), lambda b, j, i: (b, 0, j)),
                  pl.BlockSpec((1, nc, pair), lambda b, j, i: (b, 0, j)),
                  pl.BlockSpec((1, s_, pair), lambda b, j, i: (b, 0, j)),
                  pl.BlockSpec((1, nc, pair), lambda b, j, i: (b, 0, j))],
        out_specs=pl.BlockSpec((1, tq, LANES), lambda b, j, i: (b, i, j)),
        compiler_params=_params("parallel", "parallel", "arbitrary"),
        name="attention",
    )(q, k_lat, k_ctx, v_lat, v_ctx)


def _scan_chunk(a, b, reverse):
    tc = a.shape[0]
    row = lax.broadcasted_iota(jnp.int32, a.shape, 0)
    s = 1
    while s < tc:
        if s < SUBLANES:
            if reverse:
                keep = row < tc - s
                a_sh = jnp.where(keep, pltpu.roll(a, tc - s, 0), 1.0)
                b_sh = jnp.where(keep, pltpu.roll(b, tc - s, 0), 0.0)
            else:
                keep = row >= s
                a_sh = jnp.where(keep, pltpu.roll(a, s, 0), 1.0)
                b_sh = jnp.where(keep, pltpu.roll(b, s, 0), 0.0)
        else:
            ones = jnp.ones((s, a.shape[1]), F32)
            zeros = jnp.zeros((s, a.shape[1]), F32)
            if reverse:
                a_sh = jnp.concatenate([a[s:], ones], axis=0)
                b_sh = jnp.concatenate([b[s:], zeros], axis=0)
            else:
                a_sh = jnp.concatenate([ones, a[:tc - s]], axis=0)
                b_sh = jnp.concatenate([zeros, b[:tc - s]], axis=0)
        b = a * b_sh + b
        a = a * a_sh
        s *= 2
    return a, b


def _lru_kernel(xl_ref, xc_ref, gl_ref, cw_ref, cb_ref, wg_ref, bg_ref, lam_ref, o_ref,
                padl_ref, padc_ref, hf_ref, hb_ref):
    s_ = xl_ref.shape[1]
    nc = xc_ref.shape[1]
    tc = SCAN_CHUNK
    halo = SUBLANES
    zero_halo = jnp.zeros((halo, LANES), F32)
    padl_ref[0:halo, :] = zero_halo
    padl_ref[halo:halo + s_, :] = xl_ref[0]
    padl_ref[halo + s_:2 * halo + s_, :] = zero_halo
    padc_ref[0:halo, :] = zero_halo
    padc_ref[halo:halo + nc, :] = xc_ref[0]
    padc_ref[halo + nc:2 * halo + nc, :] = zero_halo

    cw = cw_ref[...]
    cb = cb_ref[...]

    def coeffs(pad_ref, j, d):
        win = pad_ref[pl.ds(pl.multiple_of(j * tc, SUBLANES), tc + 2 * halo), :]
        n = tc + 2 * halo
        u = (cb + cw[0:1] * pltpu.roll(win, 2, 0)[halo:halo + tc]
             + cw[1:2] * pltpu.roll(win, 1, 0)[halo:halo + tc]
             + cw[2:3] * win[halo:halo + tc]
             + cw[3:4] * pltpu.roll(win, n - 1, 0)[halo:halo + tc])
        gates = jnp.dot(u.astype(BF16), wg_ref[d, 0], preferred_element_type=F32) + bg_ref[d, 0]
        r = jax.nn.sigmoid(gates[:, :LANES])
        i = jax.nn.sigmoid(gates[:, LANES:])
        z = -lam_ref[d]
        softplus = jnp.maximum(z, 0.0) + jnp.log(1.0 + jnp.exp(-jnp.abs(z)))
        log_a = (-RG_C) * r * softplus
        a = jnp.exp(log_a)
        b = jnp.sqrt(1.0 - a * a) * (i * u)
        return a, b

    def local_scan(pad_ref, j, d):
        a, b = coeffs(pad_ref, j, d)
        return _scan_chunk(a, b, reverse=(d == 1))

    def apply_carry(scanned, d, carry):
        a_cum, b_loc = scanned
        h = a_cum * carry + b_loc
        return h, (h[0:1] if d == 1 else h[tc - 1:tc])

    cf = cb_ = jnp.zeros((1, LANES), F32)
    n_ctx = nc // tc
    for j in range(n_ctx):
        _, cf = apply_carry(local_scan(padc_ref, j, 0), 0, cf)
        _, cb_ = apply_carry(local_scan(padc_ref, n_ctx - 1 - j, 1), 1, cb_)

    n_lat = s_ // tc
    grp = min(SCAN_GROUP, n_lat)

    def body(jj, carry):
        cf, cb_ = carry
        fwd = [jj * grp + g for g in range(grp)]
        bwd = [n_lat - 1 - jj * grp - g for g in range(grp)]
        scans_f = [local_scan(padl_ref, j, 0) for j in fwd]
        scans_b = [local_scan(padl_ref, j, 1) for j in bwd]
        for j, sc in zip(fwd, scans_f):
            h, cf = apply_carry(sc, 0, cf)
            hf_ref[pl.ds(pl.multiple_of(j * tc, tc), tc), :] = h
        for j, sc in zip(bwd, scans_b):
            h, cb_ = apply_carry(sc, 1, cb_)
            hb_ref[pl.ds(pl.multiple_of(j * tc, tc), tc), :] = h
        return cf, cb_

    lax.fori_loop(0, n_lat // grp, body, (cf, cb_))
    o_ref[0] = ((hf_ref[...] + hb_ref[...]) * gl_ref[0].astype(F32)).astype(BF16)


def _rglru(lx_lat, lx_ctx, gl, conv_w, conv_b, wg, bg, lam):
    b_, s_, w = lx_lat.shape
    nc = lx_ctx.shape[1]
    ng = w // LANES
    halo = SUBLANES
    return pl.pallas_call(
        _lru_kernel,
        out_shape=jax.ShapeDtypeStruct((b_, s_, w), BF16),
        grid=(b_, ng),
        in_specs=[pl.BlockSpec((1, s_, LANES), lambda b, g: (b, 0, g)),
                  pl.BlockSpec((1, nc, LANES), lambda b, g: (b, 0, g)),
                  pl.BlockSpec((1, s_, LANES), lambda b, g: (b, 0, g)),
                  pl.BlockSpec((CONV_W, LANES), lambda b, g: (0, g)),
                  pl.BlockSpec((1, LANES), lambda b, g: (0, g)),
                  pl.BlockSpec((2, 1, LANES, 2 * LANES), lambda b, g: (0, g, 0, 0)),
                  pl.BlockSpec((2, 1, 1, 2 * LANES), lambda b, g: (0, g, 0, 0)),
                  pl.BlockSpec((2, 1, LANES), lambda b, g: (0, 0, g))],
        out_specs=pl.BlockSpec((1, s_, LANES), lambda b, g: (b, 0, g)),
        scratch_shapes=[pltpu.VMEM((s_ + 2 * halo, LANES), F32),
                        pltpu.VMEM((nc + 2 * halo, LANES), F32),
                        pltpu.VMEM((s_, LANES), F32),
                        pltpu.VMEM((s_, LANES), F32)],
        compiler_params=_params("parallel", "parallel"),
        name="rglru",
    )(lx_lat, lx_ctx, gl, conv_w, conv_b, wg, bg, lam)


def _first_index(mask, iota, limit, axis):
    return jnp.min(jnp.where(mask, iota, limit), axis=axis, keepdims=True)


def _outproj_kernel(o_ref, y2_ref, x_ref, g1_ref, sh_ref, sc_ref, g2_ref, gf_ref, wo_ref, wrt_ref, rb_ref,
                    wsgu_ref, wsd_ref, xs_ref, fp_ref, e_ref, w_ref, stage_ref):
    half = wo_ref.shape[0] // 2
    mix = (jnp.dot(o_ref[0], wo_ref[0:half, :], preferred_element_type=F32)
           + jnp.dot(y2_ref[0], wo_ref[half:, :], preferred_element_type=F32))
    x1 = x_ref[0] + g1_ref[0] * mix
    f = _rms(x1, gf_ref[...]) * (1.0 + sc_ref[0]) + sh_ref[0]
    f16 = f.astype(BF16)
    tm = f.shape[0]

    logits = lax.dot_general(wrt_ref[...], f, (((1,), (1,)), ((), ())),
                             precision=lax.Precision.HIGHEST, preferred_element_type=F32)
    scores = jax.nn.sigmoid(logits)
    sel = scores + rb_ref[...]
    per = N_EXPERTS // N_GROUPS
    g3 = sel.reshape(N_GROUPS, per, tm)
    mem = lax.broadcasted_iota(jnp.int32, g3.shape, 1)
    m1 = jnp.max(g3, axis=1, keepdims=True)
    first = _first_index(g3 == m1, mem, per, 1)
    m2 = jnp.max(jnp.where(mem == first, NEG_INF, g3), axis=1, keepdims=True)
    gscore = (m1 + m2).reshape(N_GROUPS, tm)
    giota = lax.broadcasted_iota(jnp.int32, gscore.shape, 0)
    gmask = jnp.zeros(gscore.shape, F32)
    cur = gscore
    for _ in range(TOPK_GROUPS):
        mx = jnp.max(cur, axis=0, keepdims=True)
        pick = giota == _first_index(cur == mx, giota, N_GROUPS, 0)
        gmask = jnp.where(pick, 1.0, gmask)
        cur = jnp.where(pick, NEG_INF, cur)
    allowed = jnp.broadcast_to(gmask.reshape(N_GROUPS, 1, tm), g3.shape) > 0.0
    cur = jnp.where(allowed, g3, NEG_INF).reshape(N_EXPERTS, tm)
    eiota = lax.broadcasted_iota(jnp.int32, cur.shape, 0)
    picked_w = []
    for k in range(TOP_K):
        mx = jnp.max(cur, axis=0, keepdims=True)
        idx = _first_index(cur == mx, eiota, N_EXPERTS, 0)
        pick = eiota == idx
        e_ref[k:k + 1, :] = idx
        picked_w.append(jnp.sum(jnp.where(pick, scores, 0.0), axis=0, keepdims=True))
        cur = jnp.where(pick, NEG_INF, cur)
    wsum = picked_w[0]
    for k in range(1, TOP_K):
        wsum = wsum + picked_w[k]
    for k in range(TOP_K):
        w_ref[k:k + 1, :] = ROUTED_SCALE * picked_w[k] / wsum

    a = jnp.dot(f16, wsgu_ref[...], preferred_element_type=F32)
    ff = a.shape[1] // 2
    act = _silu(a[:, :ff]) * a[:, ff:]
    shared = jnp.dot(act.astype(BF16), wsd_ref[...], preferred_element_type=F32)
    xs_ref[0] = x1 + g2_ref[0] * shared
    _rows_to_slabs(f, stage_ref, fp_ref.at[0])


def _output_projection(o, y2, x, g1, sh2, sc2, g2, gf, wts):
    b_, s_, d = x.shape
    tm = min(TOKEN_TILE, s_)
    nt = s_ // tm
    mod_spec = pl.BlockSpec((1, 1, d), lambda b, i: (b, 0, 0))

    def const(a):
        return pl.BlockSpec(a.shape, lambda b, i: (0,) * a.ndim)

    wide = lambda w: pl.BlockSpec((1, tm, w), lambda b, i: (b, i, 0))
    route_spec = pl.BlockSpec((TOP_K, tm), lambda b, i: (0, b * nt + i))
    weights = (wts["w_out"], wts["router_t"], wts["router_b"], wts["sh_gu"], wts["sh_d"])
    return pl.pallas_call(
        _outproj_kernel,
        out_shape=[jax.ShapeDtypeStruct((b_, s_, d), F32),
                   jax.ShapeDtypeStruct((b_, s_ * SUBLANES, LANES), F32),
                   jax.ShapeDtypeStruct((TOP_K, b_ * s_), jnp.int32),
                   jax.ShapeDtypeStruct((TOP_K, b_ * s_), F32)],
        grid=(b_, nt),
        in_specs=[wide(o.shape[2]), wide(y2.shape[2]), wide(d), mod_spec, mod_spec, mod_spec, mod_spec,
                  const(gf)] + [const(w) for w in weights],
        out_specs=[wide(d), pl.BlockSpec((1, tm * SUBLANES, LANES), lambda b, i: (b, i, 0)), route_spec, route_spec],
        scratch_shapes=[pltpu.VMEM((SUBLANES * (tm + 1), LANES), F32)],
        compiler_params=_params("parallel", "parallel"),
        name="output_projection",
    )(o, y2, x, g1, sh2, sc2, g2, gf, *weights)


def _moe_kernel(tile_e_ref, tile_src_ref, tile_n_ref, tile_slot_ref, tile_first_ref, tile_next_ref, ntiles_ref,
                tok_ref, wl_ref, fp_ref, wgu_hbm, wd_hbm, acc_ref,
                tin_a, tin_b, tout_a, tout_b, wgu_buf, wd_buf, sem):
    m = EXPERT_TILE
    st = TILE_STRIDE
    n_out = tout_a.shape[0] // st

    def slab(off):
        return pl.ds(pl.multiple_of(off, SUBLANES), SUBLANES)

    def gather(tile, tin):
        base = tile_src_ref[tile]
        for mi in range(m):
            tin[pl.ds(mi, SUBLANES, stride=st), :] = fp_ref[slab(tok_ref[base + mi]), :]

    def weight_copies(e, slot):
        return (pltpu.make_async_copy(wgu_hbm.at[e], wgu_buf.at[slot], sem.at[0, slot]),
                pltpu.make_async_copy(wd_hbm.at[e], wd_buf.at[slot], sem.at[1, slot]))

    def weights_ready(tile):
        @pl.when(tile_first_ref[tile] == 1)
        def _():
            slot = tile_slot_ref[tile]
            for cp in weight_copies(tile_e_ref[tile], slot):
                cp.wait()
            nxt = tile_next_ref[tile]

            @pl.when(nxt >= 0)
            def _():
                for cp in weight_copies(nxt, 1 - slot):
                    cp.start()

    def experts(tile, tin, tout):
        slot = tile_slot_ref[tile]
        xt = jnp.concatenate([tin[j * st:j * st + m, :].astype(BF16) for j in range(n_out)], axis=-1)
        hcat = jnp.dot(xt, wgu_buf[slot], preferred_element_type=F32)
        ff = hcat.shape[1] // 2
        act = _silu(hcat[:, :ff]) * hcat[:, ff:]
        y = jnp.dot(act.astype(BF16), wd_buf[slot], preferred_element_type=F32)
        valid = lax.broadcasted_iota(jnp.int32, (m, LANES), 0) < tile_n_ref[tile]
        for j in range(n_out):
            tout[j * st:j * st + m, :] = jnp.where(valid, y[:, j * LANES:(j + 1) * LANES], 0.0)

    def scatter(tile, tout):
        base = tile_src_ref[tile]
        for g in range(m // SCATTER_UNROLL):
            pend = []
            for r in range(SCATTER_UNROLL):
                row = g * SCATTER_UNROLL + r
                dst = slab(tok_ref[base + row])
                pend.append((dst, acc_ref[dst, :] + wl_ref[base + row] * tout[pl.ds(row, SUBLANES, stride=st), :]))
            for dst, new in reversed(pend):
                acc_ref[dst, :] = new

    for cp in weight_copies(tile_e_ref[0], 0):
        cp.start()
    acc_ref[...] = jnp.zeros(acc_ref.shape, F32)
    tout_b[...] = jnp.zeros(tout_b.shape, F32)
    gather(0, tin_a)

    def iteration(t, tin_cur, tin_nxt, tout_cur, tout_prev):
        weights_ready(t)
        gather(t + 1, tin_nxt)
        experts(t, tin_cur, tout_cur)
        scatter(jnp.maximum(t - 1, 0), tout_prev)

    def body(i, carry):
        iteration(2 * i, tin_a, tin_b, tout_a, tout_b)
        iteration(2 * i + 1, tin_b, tin_a, tout_b, tout_a)
        return carry

    lax.fori_loop(0, (ntiles_ref[0] + 2) // 2, body, 0)


def _routed_experts(tables, tok, wl, buf, wgu, wd, b):
    d = wd.shape[2]
    _, rows, _ = buf.shape
    assert d == SUBLANES * LANES
    stage = d // LANES * TILE_STRIDE
    block = pl.BlockSpec((None, rows, LANES), lambda i, *_: (b, 0, 0), pipeline_mode=pl.Buffered(1))
    hbm = pl.BlockSpec(memory_space=pl.ANY)
    n_prefetch = len(tables) + 2
    grid_spec = pltpu.PrefetchScalarGridSpec(
        num_scalar_prefetch=n_prefetch,
        grid=(1,),
        in_specs=[block, hbm, hbm],
        out_specs=block,
        scratch_shapes=[pltpu.VMEM((stage, LANES), F32) for _ in range(4)]
        + [pltpu.VMEM((2,) + wgu.shape[1:], wgu.dtype), pltpu.VMEM((2,) + wd.shape[1:], wd.dtype),
           pltpu.SemaphoreType.DMA((2, 2))],
    )
    return pl.pallas_call(
        _moe_kernel,
        out_shape=jax.ShapeDtypeStruct(buf.shape, F32),
        grid_spec=grid_spec,
        input_output_aliases={n_prefetch: 0},
        compiler_params=_params("arbitrary"),
        name="routed_experts",
    )(*tables, tok, wl, buf, wgu, wd)


def _final_kernel(xs_ref, r_ref, g2_ref, g_ref, o_ref, stage_ref):
    routed = _slabs_to_rows(r_ref, stage_ref, xs_ref.shape[1])
    o_ref[0] = _rms(xs_ref[0] + g2_ref[0] * routed, g_ref[...])


def _final(xs, routed, g2, g):
    b_, s_, d = xs.shape
    tm = min(TOKEN_TILE, s_)
    nt = s_ // tm
    wide = pl.BlockSpec((1, tm, d), lambda b, i: (b, i, 0))
    return pl.pallas_call(
        _final_kernel,
        out_shape=jax.ShapeDtypeStruct((b_, s_, d), F32),
        grid=(b_, nt),
        in_specs=[wide, pl.BlockSpec((tm * SUBLANES, LANES), lambda b, i: (b * nt + i, 0)),
                  pl.BlockSpec((1, 1, d), lambda b, i: (b, 0, 0)),
                  pl.BlockSpec((1, d), lambda b, i: (0, 0))],
        out_specs=wide,
        scratch_shapes=[pltpu.VMEM((SUBLANES * (tm + 1), LANES), F32)],
        compiler_params=_params("parallel", "parallel"),
        name="final_norm",
    )(xs, routed, g2, g)


def _prep_weights(w_in, q_norm_g, w_q_up, kv_norm_g, w_kv_up, lru_w_a, lru_b_a, lru_w_x, lru_b_x, w_out,
                  router_w, router_bias, exp_w_gate, exp_w_up, exp_w_down, sh_w_gate, sh_w_up, sh_w_down):
    d = w_in.shape[0]
    h_ = MLA_HEADS
    pad_kr = jnp.zeros((d, LANES - MLA_ROPE), F32)
    w_in_p = jnp.concatenate([w_in[:, :COL_LRU_X], pad_kr, w_in[:, COL_LRU_X:]], axis=1).astype(BF16)

    wq = w_q_up.reshape(MLA_Q_RANK, h_, MLA_NOPE + MLA_ROPE)
    nope, rope = wq[:, :, :MLA_NOPE], wq[:, :, MLA_NOPE:]
    rope_sw = rope.reshape(MLA_Q_RANK, h_, 2, 2, ROPE_AXIS // 2)[:, :, :, ::-1, :].reshape(rope.shape)
    zpad = jnp.zeros((MLA_Q_RANK, h_, LANES - MLA_NOPE - MLA_ROPE), F32)
    wq_p = jnp.concatenate([nope, rope, zpad], axis=-1).reshape(MLA_Q_RANK, h_ * LANES).astype(BF16)
    wq_sw = jnp.concatenate([jnp.zeros_like(nope), rope_sw, zpad], axis=-1).reshape(MLA_Q_RANK, h_ * LANES).astype(BF16)

    wkv = w_kv_up.reshape(MLA_KV_RANK, h_, MLA_NOPE + MLA_V)
    k_nope, v_w = wkv[:, :, :MLA_NOPE], wkv[:, :, MLA_NOPE:]
    r_idx = jnp.arange(MLA_ROPE)
    place = jnp.zeros((LANES, h_, LANES), F32).at[r_idx, :, MLA_NOPE + r_idx].set(1.0)
    place_sw = jnp.zeros((LANES, h_, LANES), F32).at[r_idx ^ (ROPE_AXIS // 2), :, MLA_NOPE + r_idx].set(1.0)
    k_top = jnp.concatenate([k_nope, jnp.zeros((MLA_KV_RANK, h_, LANES - MLA_NOPE), F32)], axis=-1)
    wk = jnp.concatenate([k_top, place], axis=0).reshape(MLA_KV_RANK + LANES, h_ * LANES).astype(BF16)
    wk_sw = jnp.concatenate([jnp.zeros_like(k_top), place_sw], axis=0).reshape(MLA_KV_RANK + LANES, h_ * LANES).astype(BF16)
    zv = jnp.zeros_like(v_w)
    even = (jnp.arange(h_) % 2 == 0)[None, :, None]
    wv = jnp.concatenate([jnp.where(even, v_w, zv), jnp.where(even, zv, v_w)], axis=-1)
    wv = wv.reshape(MLA_KV_RANK, h_ * LANES).astype(BF16)
    one_lo = jnp.concatenate([jnp.zeros((MLA_V,), F32), jnp.ones((LANES - MLA_V,), F32)])
    one_hi = jnp.concatenate([jnp.ones((LANES - MLA_V,), F32), jnp.zeros((MLA_V,), F32)])
    v_ones = jnp.where(even[0], one_lo[None, :], one_hi[None, :]).reshape(1, h_ * LANES)

    eye = jnp.eye(LRU_BLOCKS, dtype=F32)
    ng = LRU_WIDTH // LANES

    def dense(w):
        return jnp.einsum("xncd,nm->xncmd", w, eye).reshape(2, LRU_WIDTH, LRU_WIDTH)

    def grp(wd_):
        return jnp.stack([wd_[:, g * LANES:(g + 1) * LANES, g * LANES:(g + 1) * LANES] for g in range(ng)], axis=1)

    wg = jnp.concatenate([grp(dense(lru_w_a)), grp(dense(lru_w_x))], axis=-1).astype(BF16)
    bg = jnp.concatenate([lru_b_a.reshape(2, ng, 1, LANES), lru_b_x.reshape(2, ng, 1, LANES)], axis=-1)

    return dict(
        w_in=w_in_p, q_g=q_norm_g[None], kv_g=kv_norm_g[None], wq=wq_p, wq_sw=wq_sw, wk=wk, wk_sw=wk_sw, wv=wv,
        v_ones=v_ones,
        wg=wg, bg=bg,
        w_out=w_out.astype(BF16), router_t=router_w.T, router_b=router_bias[:, None],
        sh_gu=jnp.concatenate([sh_w_gate, sh_w_up], axis=1).astype(BF16), sh_d=sh_w_down.astype(BF16),
        exp_gu=jnp.concatenate([exp_w_gate, exp_w_up], axis=2).astype(BF16), exp_d=exp_w_down.astype(BF16),
    )


def _rope_tables(s_, nc):
    rows = s_ // GRID_W
    row = jnp.repeat(jnp.arange(rows, dtype=F32), GRID_W)
    col = jnp.tile(jnp.arange(GRID_W, dtype=F32), rows)
    inv_freq = ROPE_BASE ** (-jnp.arange(0, ROPE_AXIS, 2, dtype=F32) / ROPE_AXIS)
    ang = jnp.stack([row, col], axis=-1)[:, :, None] * inv_freq
    cos = jnp.broadcast_to(jnp.cos(ang)[:, :, None, :], (s_, 2, 2, ROPE_AXIS // 2)).reshape(s_, MLA_ROPE)
    sin = jnp.sin(ang)[:, :, None, :] * jnp.array([-1.0, 1.0], F32)[None, None, :, None]
    sin = sin.reshape(s_, MLA_ROPE)
    ones = jnp.ones((s_, MLA_NOPE), F32)
    zeros = jnp.zeros((s_, LANES - MLA_NOPE - MLA_ROPE), F32)
    c_tab = jnp.concatenate([ones, cos, zeros], axis=1)
    s_tab = jnp.concatenate([jnp.zeros_like(ones), sin, zeros], axis=1)
    c_ctx = jnp.concatenate([jnp.ones((nc, MLA_NOPE + MLA_ROPE), F32),
                             jnp.zeros((nc, LANES - MLA_NOPE - MLA_ROPE), F32)], axis=1)
    return c_tab, s_tab, c_ctx, jnp.zeros_like(c_ctx)


def _dispatch(eidx, wts, b_, s_):
    n = TOP_K * s_
    e = eidx.reshape(TOP_K, b_, s_).transpose(1, 0, 2).reshape(b_, n)
    w = wts.reshape(TOP_K, b_, s_).transpose(1, 0, 2).reshape(b_, n)
    key = e * n + jnp.arange(n, dtype=jnp.int32)[None, :]
    key_sorted, w_sorted = lax.sort((key, w), dimension=1, num_keys=1)
    tok = ((key_sorted % n) % s_) * SUBLANES
    tok = jnp.pad(tok.astype(jnp.int32), ((0, 0), (0, EXPERT_TILE)))
    w_sorted = jnp.pad(w_sorted, ((0, 0), (0, EXPERT_TILE)))

    ids = jnp.arange(N_EXPERTS, dtype=jnp.int32)
    counts = jnp.sum(e[:, :, None] == ids[None, None, :], axis=1, dtype=jnp.int32)
    starts = jnp.cumsum(counts, axis=1) - counts
    tiles = (counts + EXPERT_TILE - 1) // EXPERT_TILE
    tile_end = jnp.cumsum(tiles, axis=1)
    ntiles = tile_end[:, -1:]
    tid = jnp.arange(n // EXPERT_TILE + N_EXPERTS + 3, dtype=jnp.int32)[None, :]
    live = tid < ntiles
    tile_e = jnp.sum(tid[:, :, None] >= tile_end[:, None, :], axis=-1, dtype=jnp.int32)
    last_e = jnp.max(jnp.where(tiles > 0, ids[None, :], 0), axis=1, keepdims=True)
    tile_e = jnp.where(live, tile_e, last_e)
    take = lambda a: jnp.take_along_axis(a, tile_e, axis=1)
    within = (tid - (take(tile_end) - take(tiles))) * EXPERT_TILE
    tile_src = jnp.where(live, take(starts) + within, 0)
    tile_n = jnp.where(live, jnp.clip(take(counts) - within, 0, EXPERT_TILE), 0)
    rank = jnp.cumsum((tiles > 0).astype(jnp.int32), axis=1) - 1
    tile_slot = take(rank) % 2
    tile_first = (live & (within == 0)).astype(jnp.int32)
    later = lax.cummin(jnp.where(tiles > 0, ids[None, :], N_EXPERTS), axis=1, reverse=True)
    nxt = jnp.concatenate([later[:, 1:], jnp.full((b_, 1), N_EXPERTS, jnp.int32)], axis=1)
    tile_next = take(jnp.where(nxt < N_EXPERTS, nxt, -1))
    return (tile_e, tile_src, tile_n, tile_slot, tile_first, tile_next, ntiles), tok, w_sorted


def kernel(x, c, ctx, c_ctx, w_mod, b_mod, norm_mix_g, w_in, q_norm_g, w_q_up, kv_norm_g, w_kv_up, conv_w, conv_b,
           lru_w_a, lru_b_a, lru_w_x, lru_b_x, lru_lambda, w_out, norm_ffn_g, router_w, router_bias, exp_w_gate,
           exp_w_up, exp_w_down, sh_w_gate, sh_w_up, sh_w_down, final_norm_g):
    b_, s_, d = x.shape
    nc = ctx.shape[1]
    assert w_mod.shape[0] == 1, "single-layer operation"
    assert s_ % GRID_W == 0 and s_ % SCAN_CHUNK == 0 and nc % SCAN_CHUNK == 0
    assert s_ % min(TOKEN_TILE, s_) == 0 and s_ % min(Q_TILE, s_) == 0

    wts = _prep_weights(w_in[0], q_norm_g[0], w_q_up[0], kv_norm_g[0], w_kv_up[0], lru_w_a[0], lru_b_a[0],
                        lru_w_x[0], lru_b_x[0], w_out[0], router_w[0], router_bias[0], exp_w_gate[0], exp_w_up[0],
                        exp_w_down[0], sh_w_gate[0], sh_w_up[0], sh_w_down[0])
    c_tab, s_tab, c_ctx_tab, s_ctx_tab = _rope_tables(s_, nc)

    rows = (b_ + 1 + SUBLANES - 1) // SUBLANES * SUBLANES
    cc = jnp.zeros((rows, d), F32).at[:b_].set(c).at[b_].set(c_ctx)
    mod = _modulation(cc, w_mod[0], b_mod[0][None])
    sh1, sc1, g1, sh2, sc2, g2 = [mod[:b_, i * d:(i + 1) * d].reshape(b_, 1, d) for i in range(6)]
    csh1, csc1 = [mod[b_, i * d:(i + 1) * d].reshape(1, 1, d) for i in range(2)]

    q, k_lat, v_lat, lx_lat, gl = _input_projection(
        x, sh1, sc1, norm_mix_g, wts, (c_tab * ATTN_SCALE, s_tab * ATTN_SCALE, c_tab, s_tab), True)
    k_ctx, v_ctx, lx_ctx, _ = _input_projection(
        ctx, csh1, csc1, norm_mix_g, wts, (c_ctx_tab, s_ctx_tab, c_ctx_tab, s_ctx_tab), False)

    o_lat = _attention(q, k_lat, k_ctx, v_lat, v_ctx)
    y2 = _rglru(lx_lat, lx_ctx, gl, conv_w[0], conv_b[0][None], wts["wg"], wts["bg"],
                lru_lambda[0].reshape(2, 1, LRU_WIDTH))

    xs, fp, eidx, rw = _output_projection(o_lat, y2, x, g1, sh2, sc2, g2, norm_ffn_g, wts)

    tables, tok, wl = _dispatch(eidx, rw, b_, s_)
    buf = fp
    for b in range(b_):
        buf = _routed_experts([tb[b] for tb in tables], tok[b], wl[b], buf, wts["exp_gu"], wts["exp_d"], b)
    routed = buf.reshape(b_ * s_ * SUBLANES, LANES)

    return _final(xs, routed, g2, final_norm_g[None])
```

```python
import functools

import jax
import jax.numpy as jnp
from jax import lax
from jax.experimental import pallas as pl
from jax.experimental.pallas import tpu as pltpu

GRID_W = 64
EPS = 1e-6
MLA_HEADS = 8
MLA_NOPE = 64
MLA_ROPE = 32
MLA_V = 64
MLA_Q_RANK = 256
MLA_KV_RANK = 128
LRU_WIDTH = 512
LRU_BLOCKS = 8
CONV_W = 4
RG_C = 8.0
ROPE_BASE = 10000.0
ROPE_AXIS = MLA_ROPE // 2
ATTN_SCALE = (MLA_NOPE + MLA_ROPE) ** -0.5
N_EXPERTS = 64
TOP_K = 8
N_GROUPS = 8
TOPK_GROUPS = 4
EXPERT_FF = 256
ROUTED_SCALE = 2.5
COL_KV = MLA_Q_RANK
COL_KR = COL_KV + MLA_KV_RANK
COL_LRU_X = COL_KR + MLA_ROPE
COL_LRU_G = COL_LRU_X + LRU_WIDTH

LANES = 128
SUBLANES = 8
VMEM_LIMIT = 56 * 1024 * 1024

TOKEN_TILE = 512
Q_TILE = 1024
KEY_CHUNK = 512
SCAN_CHUNK = 128
SCAN_GROUP = 4
EXPERT_TILE = 512
TILE_STRIDE = EXPERT_TILE + 1
SCATTER_UNROLL = 8

F32 = jnp.float32
BF16 = jnp.bfloat16
NEG_INF = float("-inf")


def _params(*sem):
    return pltpu.CompilerParams(dimension_semantics=sem, vmem_limit_bytes=VMEM_LIMIT)


def _rms(t, g):
    return t * lax.rsqrt(jnp.mean(t * t, axis=-1, keepdims=True) + EPS) * g


def _gelu_tanh(t):
    return 0.5 * t * (1.0 + jnp.tanh(0.7978845608028654 * (t + 0.044715 * (t * t * t))))


def _silu(t):
    return t * jax.nn.sigmoid(t)


def _rows_to_slabs(rows, stage_ref, slab_ref):
    n = rows.shape[0]
    st = n + 1
    for j in range(SUBLANES):
        stage_ref[j * st:j * st + n, :] = rows[:, j * LANES:(j + 1) * LANES]
    for r in range(n):
        slab_ref[r * SUBLANES:(r + 1) * SUBLANES, :] = stage_ref[pl.ds(r, SUBLANES, stride=st), :]


def _slabs_to_rows(slab_ref, stage_ref, n):
    st = n + 1
    for r in range(n):
        stage_ref[pl.ds(r, SUBLANES, stride=st), :] = slab_ref[r * SUBLANES:(r + 1) * SUBLANES, :]
    return jnp.concatenate([stage_ref[j * st:j * st + n, :] for j in range(SUBLANES)], axis=-1)


def _nt_dot(a, b):
    return lax.dot_general(a, b, (((1,), (1,)), ((), ())), preferred_element_type=F32)


def _mod_kernel(c_ref, w_ref, b_ref, o_ref):
    s = _silu(c_ref[...])
    o_ref[...] = jnp.dot(s, w_ref[...], precision=lax.Precision.HIGHEST,
                         preferred_element_type=F32) + b_ref[...]


def _modulation(cc, w, b):
    rows, d = cc.shape
    n = w.shape[1]
    tn = 1024
    return pl.pallas_call(
        _mod_kernel,
        out_shape=jax.ShapeDtypeStruct((rows, n), F32),
        grid=(n // tn,),
        in_specs=[pl.BlockSpec((rows, d), lambda j: (0, 0)),
                  pl.BlockSpec((d, tn), lambda j: (0, j)),
                  pl.BlockSpec((1, tn), lambda j: (0, j))],
        out_specs=pl.BlockSpec((rows, tn), lambda j: (0, j)),
        compiler_params=_params("arbitrary"),
        name="modulation",
    )(cc, w, b)


def _inproj_kernel(*refs, with_q):
    (x_ref, sh_ref, sc_ref, g_ref, win_ref, qg_ref, kvg_ref, wq_ref, wqs_ref, wk_ref, wks_ref,
     wv_ref, vo_ref, cq_ref, sq_ref, ck_ref, sk_ref) = refs[:17]
    if with_q:
        q_ref, k_ref, v_ref, lx_ref, lg_ref = refs[17:]
    else:
        k_ref, v_ref, lx_ref, lg_ref = refs[17:]
    h = _rms(x_ref[0], g_ref[...]) * (1.0 + sc_ref[0]) + sh_ref[0]
    p = jnp.dot(h.astype(BF16), win_ref[...], preferred_element_type=F32)

    if with_q:
        qn = _rms(p[:, :MLA_Q_RANK], qg_ref[...]).astype(BF16)
        qa = jnp.dot(qn, wq_ref[...], preferred_element_type=F32)
        qb = jnp.dot(qn, wqs_ref[...], preferred_element_type=F32)
        cq, sq = cq_ref[...], sq_ref[...]
        for hd in range(MLA_HEADS):
            sl = slice(hd * LANES, (hd + 1) * LANES)
            q_ref[0, :, sl] = (qa[:, sl] * cq + qb[:, sl] * sq).astype(BF16)

    kvn = _rms(p[:, COL_KV:COL_KR], kvg_ref[...])
    kvn16 = kvn.astype(BF16)
    kin = jnp.concatenate([kvn16, p[:, COL_KR:COL_KR + LANES].astype(BF16)], axis=-1)
    ka = jnp.dot(kin, wk_ref[...], preferred_element_type=F32)
    kb = jnp.dot(kin, wks_ref[...], preferred_element_type=F32)
    ck, sk = ck_ref[...], sk_ref[...]
    for hd in range(MLA_HEADS):
        sl = slice(hd * LANES, (hd + 1) * LANES)
        k_ref[0, :, sl] = (ka[:, sl] * ck + kb[:, sl] * sk).astype(BF16)
    v_ref[0] = (jnp.dot(kvn16, wv_ref[...], preferred_element_type=F32) + vo_ref[...]).astype(BF16)
    lx_ref[0] = p[:, 512:512 + LRU_WIDTH]
    lg_ref[0] = _gelu_tanh(p[:, 1024:1024 + LRU_WIDTH]).astype(BF16)


def _input_projection(x, sh, sc, g, wts, tabs, with_q):
    b_, s_, d = x.shape
    tm = min(TOKEN_TILE, s_)
    hw = MLA_HEADS * LANES
    per_batch = sh.shape[0] == b_
    mod_spec = pl.BlockSpec((1, 1, d), (lambda b, i: (b, 0, 0)) if per_batch else (lambda b, i: (0, 0, 0)))

    def const(a):
        return pl.BlockSpec(a.shape, lambda b, i: (0,) * a.ndim)

    tab_spec = pl.BlockSpec((tm, LANES), lambda b, i: (i, 0))
    wide = lambda w: pl.BlockSpec((1, tm, w), lambda b, i: (b, i, 0))
    weights = (wts["w_in"], wts["q_g"], wts["kv_g"], wts["wq"], wts["wq_sw"], wts["wk"], wts["wk_sw"], wts["wv"],
               wts["v_ones"])
    out_shape = [jax.ShapeDtypeStruct((b_, s_, hw), BF16),
                 jax.ShapeDtypeStruct((b_, s_, hw), BF16),
                 jax.ShapeDtypeStruct((b_, s_, LRU_WIDTH), F32),
                 jax.ShapeDtypeStruct((b_, s_, LRU_WIDTH), BF16)]
    out_specs = [wide(hw), wide(hw), wide(LRU_WIDTH), wide(LRU_WIDTH)]
    if with_q:
        out_shape = [jax.ShapeDtypeStruct((b_, s_, hw), BF16)] + out_shape
        out_specs = [wide(hw)] + out_specs
    return pl.pallas_call(
        functools.partial(_inproj_kernel, with_q=with_q),
        out_shape=out_shape,
        grid=(b_, s_ // tm),
        in_specs=[wide(d), mod_spec, mod_spec, const(g)] + [const(w) for w in weights] + [tab_spec] * 4,
        out_specs=out_specs,
        compiler_params=_params("parallel", "parallel"),
        name="input_projection_lat" if with_q else "input_projection_ctx",
    )(x, sh, sc, g, *weights, *tabs)


def _attn_kernel(q_ref, kl_ref, kc_ref, vl_ref, vc_ref, o_ref):
    s_ = kl_ref.shape[1]
    kc = min(KEY_CHUNK, s_)
    half = LANES // 2
    lane = lax.broadcasted_iota(jnp.int32, (q_ref.shape[1], LANES), 1)
    chunks = [(kl_ref, vl_ref, c * kc, kc) for c in range(s_ // kc)] + [(kc_ref, vc_ref, 0, kc_ref.shape[1])]
    items = [(a, ch) for a in range(2) for ch in chunks]

    def scores(item):
        a, (k_ref, _, start, size) = item
        sl = slice(a * LANES, (a + 1) * LANES)
        return _nt_dot(q_ref[0, :, sl], k_ref[0, start:start + size, sl])

    out = None
    s_next = scores(items[0])
    for i, (a, (_, v_ref, start, size)) in enumerate(items):
        s = s_next
        if i + 1 < len(items):
            s_next = scores(items[i + 1])
        first = i % len(chunks) == 0
        m_c = jnp.max(s, axis=-1, keepdims=True)
        m_new = m_c if first else jnp.maximum(m, m_c)
        p = jnp.exp(s - m_new).astype(BF16)
        pv = jnp.dot(p, v_ref[0, start:start + size, a * LANES:(a + 1) * LANES], preferred_element_type=F32)
        acc = pv if first else acc * jnp.exp(m - m_new) + pv
        m = m_new
        if (i + 1) % len(chunks) == 0:
            own = (lane < half) if a == 0 else (lane >= half)
            row_sum = pltpu.roll(acc, half, 1)
            o = jnp.where(own, acc / row_sum, 0.0)
            out = o if out is None else out + o
    o_ref[0] = out.astype(BF16)


def _attention(q, k_lat, k_ctx, v_lat, v_ctx):
    b_, s_, hw = q.shape
    nc = k_ctx.shape[1]
    tq = min(Q_TILE, s_)
    pair = 2 * LANES
    n_pairs = hw // pair
    return pl.pallas_call(
        _attn_kernel,
        out_shape=jax.ShapeDtypeStruct((b_, s_, n_pairs * LANES), BF16),
        grid=(b_, n_pairs, s_ // tq),
        in_specs=[pl.BlockSpec((1, tq, pair), lambda b, j, i: (b, i, j)),
                  pl.BlockSpec((1, s_, pair), lambda b, j, i: (b, 0, j)),
                  pl.BlockSpec((1, nc, pair), lambda b, j, i: (b, 0, j)),
                  pl.BlockSpec((1, s_, pair), lambda b, j, i: (b, 0, j)),
                  pl.BlockSpec((1, nc, pair), lambda b, j, i: (b, 0, j))],
        out_specs=pl.BlockSpec((1, tq, LANES), lambda b, j, i: (b, i, j)),
        compiler_params=_params("parallel", "parallel", "arbitrary"),
        name="attention",
    )(q, k_lat, k_ctx, v_lat, v_ctx)


def _scan_chunk(a, b, reverse):
    tc = a.shape[0]
    row = lax.broadcasted_iota(jnp.int32, a.shape, 0)
    s = 1
    while s < tc:
        if s < SUBLANES:
            if reverse:
                keep = row < tc - s
                a_sh = jnp.where(keep, pltpu.roll(a, tc - s, 0), 1.0)
                b_sh = jnp.where(keep, pltpu.roll(b, tc - s, 0), 0.0)
            else:
                keep = row >= s
                a_sh = jnp.where(keep, pltpu.roll(a, s, 0), 1.0)
                b_sh = jnp.where(keep, pltpu.roll(b, s, 0), 0.0)
        else:
            ones = jnp.ones((s, a.shape[1]), F32)
            zeros = jnp.zeros((s, a.shape[1]), F32)
            if reverse:
                a_sh = jnp.concatenate([a[s:], ones], axis=0)
                b_sh = jnp.concatenate([b[s:], zeros], axis=0)
            else:
                a_sh = jnp.concatenate([ones, a[:tc - s]], axis=0)
                b_sh = jnp.concatenate([zeros, b[:tc - s]], axis=0)
        b = a * b_sh + b
        a = a * a_sh
        s *= 2
    return a, b


def _lru_kernel(xl_ref, xc_ref, gl_ref, cw_ref, cb_ref, wg_ref, bg_ref, lam_ref, o_ref,
                padl_ref, padc_ref, hf_ref, hb_ref):
    s_ = xl_ref.shape[1]
    nc = xc_ref.shape[1]
    tc = SCAN_CHUNK
    halo = SUBLANES
    zero_halo = jnp.zeros((halo, LANES), F32)
    padl_ref[0:halo, :] = zero_halo
    padl_ref[halo:halo + s_, :] = xl_ref[0]
    padl_ref[halo + s_:2 * halo + s_, :] = zero_halo
    padc_ref[0:halo, :] = zero_halo
    padc_ref[halo:halo + nc, :] = xc_ref[0]
    padc_ref[halo + nc:2 * halo + nc, :] = zero_halo

    cw = cw_ref[...]
    cb = cb_ref[...]

    def coeffs(pad_ref, j, d):
        win = pad_ref[pl.ds(pl.multiple_of(j * tc, SUBLANES), tc + 2 * halo), :]
        n = tc + 2 * halo
        u = (cb + cw[0:1] * pltpu.roll(win, 2, 0)[halo:halo + tc]
             + cw[1:2] * pltpu.roll(win, 1, 0)[halo:halo + tc]
             + cw[2:3] * win[halo:halo + tc]
             + cw[3:4] * pltpu.roll(win, n - 1, 0)[halo:halo + tc])
        gates = jnp.dot(u.astype(BF16), wg_ref[d, 0], preferred_element_type=F32) + bg_ref[d, 0]
        r = jax.nn.sigmoid(gates[:, :LANES])
        i = jax.nn.sigmoid(gates[:, LANES:])
        z = -lam_ref[d]
        softplus = jnp.maximum(z, 0.0) + jnp.log(1.0 + jnp.exp(-jnp.abs(z)))
        log_a = (-RG_C) * r * softplus
        a = jnp.exp(log_a)
        b = jnp.sqrt(1.0 - a * a) * (i * u)
        return a, b

    def local_scan(pad_ref, j, d):
        a, b = coeffs(pad_ref, j, d)
        return _scan_chunk(a, b, reverse=(d == 1))

    def apply_carry(scanned, d, carry):
        a_cum, b_loc = scanned
        h = a_cum * carry + b_loc
        return h, (h[0:1] if d == 1 else h[tc - 1:tc])

    cf = cb_ = jnp.zeros((1, LANES), F32)
    n_ctx = nc // tc
    for j in range(n_ctx):
        _, cf = apply_carry(local_scan(padc_ref, j, 0), 0, cf)
        _, cb_ = apply_carry(local_scan(padc_ref, n_ctx - 1 - j, 1), 1, cb_)

    n_lat = s_ // tc
    grp = min(SCAN_GROUP, n_lat)

    def body(jj, carry):
        cf, cb_ = carry
        fwd = [jj * grp + g for g in range(grp)]
        bwd = [n_lat - 1 - jj * grp - g for g in range(grp)]
        scans_f = [local_scan(padl_ref, j, 0) for j in fwd]
        scans_b = [local_scan(padl_ref, j, 1) for j in bwd]
        for j, sc in zip(fwd, scans_f):
            h, cf = apply_carry(sc, 0, cf)
            hf_ref[pl.ds(pl.multiple_of(j * tc, tc), tc), :] = h
        for j, sc in zip(bwd, scans_b):
            h, cb_ = apply_carry(sc, 1, cb_)
            hb_ref[pl.ds(pl.multiple_of(j * tc, tc), tc), :] = h
        return cf, cb_

    lax.fori_loop(0, n_lat // grp, body, (cf, cb_))
    o_ref[0] = ((hf_ref[...] + hb_ref[...]) * gl_ref[0].astype(F32)).astype(BF16)


def _rglru(lx_lat, lx_ctx, gl, conv_w, conv_b, wg, bg, lam):
    b_, s_, w = lx_lat.shape
    nc = lx_ctx.shape[1]
    ng = w // LANES
    halo = SUBLANES
    return pl.pallas_call(
        _lru_kernel,
        out_shape=jax.ShapeDtypeStruct((b_, s_, w), BF16),
        grid=(b_, ng),
        in_specs=[pl.BlockSpec((1, s_, LANES), lambda b, g: (b, 0, g)),
                  pl.BlockSpec((1, nc, LANES), lambda b, g: (b, 0, g)),
                  pl.BlockSpec((1, s_, LANES), lambda b, g: (b, 0, g)),
                  pl.BlockSpec((CONV_W, LANES), lambda b, g: (0, g)),
                  pl.BlockSpec((1, LANES), lambda b, g: (0, g)),
                  pl.BlockSpec((2, 1, LANES, 2 * LANES), lambda b, g: (0, g, 0, 0)),
                  pl.BlockSpec((2, 1, 1, 2 * LANES), lambda b, g: (0, g, 0, 0)),
                  pl.BlockSpec((2, 1, LANES), lambda b, g: (0, 0, g))],
        out_specs=pl.BlockSpec((1, s_, LANES), lambda b, g: (b, 0, g)),
        scratch_shapes=[pltpu.VMEM((s_ + 2 * halo, LANES), F32),
                        pltpu.VMEM((nc + 2 * halo, LANES), F32),
                        pltpu.VMEM((s_, LANES), F32),
                        pltpu.VMEM((s_, LANES), F32)],
        compiler_params=_params("parallel", "parallel"),
        name="rglru",
    )(lx_lat, lx_ctx, gl, conv_w, conv_b, wg, bg, lam)


def _first_index(mask, iota, limit, axis):
    return jnp.min(jnp.where(mask, iota, limit), axis=axis, keepdims=True)


def _outproj_kernel(o_ref, y2_ref, x_ref, g1_ref, sh_ref, sc_ref, g2_ref, gf_ref, wo_ref, wrt_ref, rb_ref,
                    wsgu_ref, wsd_ref, xs_ref, fp_ref, e_ref, w_ref, stage_ref):
    half = wo_ref.shape[0] // 2
    mix = (jnp.dot(o_ref[0], wo_ref[0:half, :], preferred_element_type=F32)
           + jnp.dot(y2_ref[0], wo_ref[half:, :], preferred_element_type=F32))
    x1 = x_ref[0] + g1_ref[0] * mix
    f = _rms(x1, gf_ref[...]) * (1.0 + sc_ref[0]) + sh_ref[0]
    f16 = f.astype(BF16)
    tm = f.shape[0]

    f_lo = (f - f16.astype(F32)).astype(BF16)
    both = _nt_dot(wrt_ref[...], f16)
    logits = both[:N_EXPERTS] + both[N_EXPERTS:] + _nt_dot(wrt_ref[0:N_EXPERTS, :], f_lo)
    scores = jax.nn.sigmoid(logits)
    sel = scores + rb_ref[...]
    per = N_EXPERTS // N_GROUPS
    g3 = sel.reshape(N_GROUPS, per, tm)
    mem = lax.broadcasted_iota(jnp.int32, g3.shape, 1)
    m1 = jnp.max(g3, axis=1, keepdims=True)
    first = _first_index(g3 == m1, mem, per, 1)
    m2 = jnp.max(jnp.where(mem == first, NEG_INF, g3), axis=1, keepdims=True)
    gscore = (m1 + m2).reshape(N_GROUPS, tm)
    giota = lax.broadcasted_iota(jnp.int32, gscore.shape, 0)
    gmask = jnp.zeros(gscore.shape, F32)
    cur = gscore
    for _ in range(TOPK_GROUPS):
        mx = jnp.max(cur, axis=0, keepdims=True)
        pick = giota == _first_index(cur == mx, giota, N_GROUPS, 0)
        gmask = jnp.where(pick, 1.0, gmask)
        cur = jnp.where(pick, NEG_INF, cur)
    allowed = jnp.broadcast_to(gmask.reshape(N_GROUPS, 1, tm), g3.shape) > 0.0
    cur = jnp.where(allowed, g3, NEG_INF).reshape(N_EXPERTS, tm)
    eiota = lax.broadcasted_iota(jnp.int32, cur.shape, 0)
    picked_w = []
    for k in range(TOP_K):
        mx = jnp.max(cur, axis=0, keepdims=True)
        idx = _first_index(cur == mx, eiota, N_EXPERTS, 0)
        pick = eiota == idx
        e_ref[k:k + 1, :] = idx
        picked_w.append(jnp.sum(jnp.where(pick, scores, 0.0), axis=0, keepdims=True))
        cur = jnp.where(pick, NEG_INF, cur)
    wsum = picked_w[0]
    for k in range(1, TOP_K):
        wsum = wsum + picked_w[k]
    for k in range(TOP_K):
        w_ref[k:k + 1, :] = ROUTED_SCALE * picked_w[k] / wsum

    a = jnp.dot(f16, wsgu_ref[...], preferred_element_type=F32)
    ff = a.shape[1] // 2
    act = _silu(a[:, :ff]) * a[:, ff:]
    shared = jnp.dot(act.astype(BF16), wsd_ref[...], preferred_element_type=F32)
    xs_ref[0] = x1 + g2_ref[0] * shared
    _rows_to_slabs(f, stage_ref, fp_ref.at[0])


def _output_projection(o, y2, x, g1, sh2, sc2, g2, gf, wts):
    b_, s_, d = x.shape
    tm = min(TOKEN_TILE, s_)
    nt = s_ // tm
    mod_spec = pl.BlockSpec((1, 1, d), lambda b, i: (b, 0, 0))

    def const(a):
        return pl.BlockSpec(a.shape, lambda b, i: (0,) * a.ndim)

    wide = lambda w: pl.BlockSpec((1, tm, w), lambda b, i: (b, i, 0))
    route_spec = pl.BlockSpec((TOP_K, tm), lambda b, i: (0, b * nt + i))
    weights = (wts["w_out"], wts["router_t"], wts["router_b"], wts["sh_gu"], wts["sh_d"])
    return pl.pallas_call(
        _outproj_kernel,
        out_shape=[jax.ShapeDtypeStruct((b_, s_, d), F32),
                   jax.ShapeDtypeStruct((b_, s_ * SUBLANES, LANES), F32),
                   jax.ShapeDtypeStruct((TOP_K, b_ * s_), jnp.int32),
                   jax.ShapeDtypeStruct((TOP_K, b_ * s_), F32)],
        grid=(b_, nt),
        in_specs=[wide(o.shape[2]), wide(y2.shape[2]), wide(d), mod_spec, mod_spec, mod_spec, mod_spec,
                  const(gf)] + [const(w) for w in weights],
        out_specs=[wide(d), pl.BlockSpec((1, tm * SUBLANES, LANES), lambda b, i: (b, i, 0)), route_spec, route_spec],
        scratch_shapes=[pltpu.VMEM((SUBLANES * (tm + 1), LANES), F32)],
        compiler_params=_params("parallel", "parallel"),
        name="output_projection",
    )(o, y2, x, g1, sh2, sc2, g2, gf, *weights)


def _moe_kernel(tile_e_ref, tile_src_ref, tile_n_ref, ntiles_ref, tok_ref, wl_ref,
                fp_ref, wgu_ref, wd_ref, acc_ref, tin_a, tin_b, tout_a, tout_b, act_a, act_b):
    t = pl.program_id(0)
    m = EXPERT_TILE
    st = TILE_STRIDE
    n_out = tout_a.shape[0] // st

    def slab(off):
        return pl.ds(pl.multiple_of(off, SUBLANES), SUBLANES)

    def gather(tile, tin):
        base = tile_src_ref[tile]
        for mi in range(m):
            tin[pl.ds(mi, SUBLANES, stride=st), :] = fp_ref[slab(tok_ref[base + mi]), :]

    def up_project(tin, act):
        xt = jnp.concatenate([tin[j * st:j * st + m, :].astype(BF16) for j in range(n_out)], axis=-1)
        hcat = jnp.dot(xt, wgu_ref[0], preferred_element_type=F32)
        ff = hcat.shape[1] // 2
        act[...] = (_silu(hcat[:, :ff]) * hcat[:, ff:]).astype(BF16)

    def down_project(tile, act, tout):
        y = jnp.dot(act[...], wd_ref[0], preferred_element_type=F32)
        valid = lax.broadcasted_iota(jnp.int32, (m, LANES), 0) < tile_n_ref[tile]
        for j in range(n_out):
            tout[j * st:j * st + m, :] = jnp.where(valid, y[:, j * LANES:(j + 1) * LANES], 0.0)

    def scatter(tile, tout):
        base = tile_src_ref[tile]
        for g in range(m // SCATTER_UNROLL):
            pend = []
            for r in range(SCATTER_UNROLL):
                row = g * SCATTER_UNROLL + r
                dst = slab(tok_ref[base + row])
                pend.append((dst, acc_ref[dst, :] + wl_ref[base + row] * tout[pl.ds(row, SUBLANES, stride=st), :]))
            for dst, new in reversed(pend):
                acc_ref[dst, :] = new

    @pl.when(t == 0)
    def _():
        acc_ref[...] = jnp.zeros(acc_ref.shape, F32)
        tout_a[...] = jnp.zeros(tout_a.shape, F32)
        act_b[...] = jnp.zeros(act_b.shape, BF16)
        gather(0, tin_a)

    def step(tin_cur, tin_nxt, act_cur, act_prev, tout_cur, tout_prev):
        gather(t + 1, tin_nxt)
        up_project(tin_cur, act_cur)
        down_project(jnp.maximum(t - 1, 0), act_prev, tout_prev)
        scatter(jnp.maximum(t - 2, 0), tout_cur)

    active = t <= ntiles_ref[0] + 1

    @pl.when(active & (t % 2 == 0))
    def _():
        step(tin_a, tin_b, act_a, act_b, tout_a, tout_b)

    @pl.when(active & (t % 2 == 1))
    def _():
        step(tin_b, tin_a, act_b, act_a, tout_b, tout_a)


def _routed_experts(tables, tok, wl, buf, wgu, wd, b):
    d = wd.shape[2]
    _, rows, _ = buf.shape
    assert d == SUBLANES * LANES
    stage = d // LANES * TILE_STRIDE
    block = pl.BlockSpec((None, rows, LANES), lambda t, *_: (b, 0, 0), pipeline_mode=pl.Buffered(1))
    n_prefetch = len(tables) + 2
    grid_spec = pltpu.PrefetchScalarGridSpec(
        num_scalar_prefetch=n_prefetch,
        grid=(tables[0].shape[0] - 1,),
        in_specs=[block,
                  pl.BlockSpec((1,) + wgu.shape[1:], lambda t, te, *_: (te[t], 0, 0)),
                  pl.BlockSpec((1,) + wd.shape[1:], lambda t, te, *_: (te[jnp.maximum(t - 1, 0)], 0, 0))],
        out_specs=block,
        scratch_shapes=[pltpu.VMEM((stage, LANES), F32) for _ in range(4)]
        + [pltpu.VMEM((EXPERT_TILE, wd.shape[1]), BF16) for _ in range(2)],
    )
    return pl.pallas_call(
        _moe_kernel,
        out_shape=jax.ShapeDtypeStruct(buf.shape, F32),
        grid_spec=grid_spec,
        input_output_aliases={n_prefetch: 0},
        compiler_params=_params("arbitrary"),
        name="routed_experts",
    )(*tables, tok, wl, buf, wgu, wd)


def _final_kernel(xs_ref, r_ref, g2_ref, g_ref, o_ref, stage_ref):
    routed = _slabs_to_rows(r_ref, stage_ref, xs_ref.shape[1])
    o_ref[0] = _rms(xs_ref[0] + g2_ref[0] * routed, g_ref[...])


def _final(xs, routed, g2, g):
    b_, s_, d = xs.shape
    tm = min(TOKEN_TILE, s_)
    nt = s_ // tm
    wide = pl.BlockSpec((1, tm, d), lambda b, i: (b, i, 0))
    return pl.pallas_call(
        _final_kernel,
        out_shape=jax.ShapeDtypeStruct((b_, s_, d), F32),
        grid=(b_, nt),
        in_specs=[wide, pl.BlockSpec((tm * SUBLANES, LANES), lambda b, i: (b * nt + i, 0)),
                  pl.BlockSpec((1, 1, d), lambda b, i: (b, 0, 0)),
                  pl.BlockSpec((1, d), lambda b, i: (0, 0))],
        out_specs=wide,
        scratch_shapes=[pltpu.VMEM((SUBLANES * (tm + 1), LANES), F32)],
        compiler_params=_params("parallel", "parallel"),
        name="final_norm",
    )(xs, routed, g2, g)


def _split_bf16(w):
    hi = w.astype(BF16)
    lo = (w - hi.astype(F32)).astype(BF16)
    return jnp.concatenate([hi, lo], axis=0)


def _prep_weights(w_in, q_norm_g, w_q_up, kv_norm_g, w_kv_up, lru_w_a, lru_b_a, lru_w_x, lru_b_x, w_out,
                  router_w, router_bias, exp_w_gate, exp_w_up, exp_w_down, sh_w_gate, sh_w_up, sh_w_down):
    d = w_in.shape[0]
    h_ = MLA_HEADS
    pad_kr = jnp.zeros((d, LANES - MLA_ROPE), F32)
    w_in_p = jnp.concatenate([w_in[:, :COL_LRU_X], pad_kr, w_in[:, COL_LRU_X:]], axis=1).astype(BF16)

    wq = w_q_up.reshape(MLA_Q_RANK, h_, MLA_NOPE + MLA_ROPE)
    nope, rope = wq[:, :, :MLA_NOPE], wq[:, :, MLA_NOPE:]
    rope_sw = rope.reshape(MLA_Q_RANK, h_, 2, 2, ROPE_AXIS // 2)[:, :, :, ::-1, :].reshape(rope.shape)
    zpad = jnp.zeros((MLA_Q_RANK, h_, LANES - MLA_NOPE - MLA_ROPE), F32)
    wq_p = jnp.concatenate([nope, rope, zpad], axis=-1).reshape(MLA_Q_RANK, h_ * LANES).astype(BF16)
    wq_sw = jnp.concatenate([jnp.zeros_like(nope), rope_sw, zpad], axis=-1).reshape(MLA_Q_RANK, h_ * LANES).astype(BF16)

    wkv = w_kv_up.reshape(MLA_KV_RANK, h_, MLA_NOPE + MLA_V)
    k_nope, v_w = wkv[:, :, :MLA_NOPE], wkv[:, :, MLA_NOPE:]
    r_idx = jnp.arange(MLA_ROPE)
    place = jnp.zeros((LANES, h_, LANES), F32).at[r_idx, :, MLA_NOPE + r_idx].set(1.0)
    place_sw = jnp.zeros((LANES, h_, LANES), F32).at[r_idx ^ (ROPE_AXIS // 2), :, MLA_NOPE + r_idx].set(1.0)
    k_top = jnp.concatenate([k_nope, jnp.zeros((MLA_KV_RANK, h_, LANES - MLA_NOPE), F32)], axis=-1)
    wk = jnp.concatenate([k_top, place], axis=0).reshape(MLA_KV_RANK + LANES, h_ * LANES).astype(BF16)
    wk_sw = jnp.concatenate([jnp.zeros_like(k_top), place_sw], axis=0).reshape(MLA_KV_RANK + LANES, h_ * LANES).astype(BF16)
    zv = jnp.zeros_like(v_w)
    even = (jnp.arange(h_) % 2 == 0)[None, :, None]
    wv = jnp.concatenate([jnp.where(even, v_w, zv), jnp.where(even, zv, v_w)], axis=-1)
    wv = wv.reshape(MLA_KV_RANK, h_ * LANES).astype(BF16)
    one_lo = jnp.concatenate([jnp.zeros((MLA_V,), F32), jnp.ones((LANES - MLA_V,), F32)])
    one_hi = jnp.concatenate([jnp.ones((LANES - MLA_V,), F32), jnp.zeros((MLA_V,), F32)])
    v_ones = jnp.where(even[0], one_lo[None, :], one_hi[None, :]).reshape(1, h_ * LANES)

    eye = jnp.eye(LRU_BLOCKS, dtype=F32)
    ng = LRU_WIDTH // LANES

    def dense(w):
        return jnp.einsum("xncd,nm->xncmd", w, eye).reshape(2, LRU_WIDTH, LRU_WIDTH)

    def grp(wd_):
        return jnp.stack([wd_[:, g * LANES:(g + 1) * LANES, g * LANES:(g + 1) * LANES] for g in range(ng)], axis=1)

    wg = jnp.concatenate([grp(dense(lru_w_a)), grp(dense(lru_w_x))], axis=-1).astype(BF16)
    bg = jnp.concatenate([lru_b_a.reshape(2, ng, 1, LANES), lru_b_x.reshape(2, ng, 1, LANES)], axis=-1)

    return dict(
        w_in=w_in_p, q_g=q_norm_g[None], kv_g=kv_norm_g[None], wq=wq_p, wq_sw=wq_sw, wk=wk, wk_sw=wk_sw, wv=wv,
        v_ones=v_ones,
        wg=wg, bg=bg,
        w_out=w_out.astype(BF16), router_t=_split_bf16(router_w.T), router_b=router_bias[:, None],
        sh_gu=jnp.concatenate([sh_w_gate, sh_w_up], axis=1).astype(BF16), sh_d=sh_w_down.astype(BF16),
        exp_gu=jnp.concatenate([exp_w_gate, exp_w_up], axis=2).astype(BF16), exp_d=exp_w_down.astype(BF16),
    )


def _rope_tables(s_, nc):
    rows = s_ // GRID_W
    row = jnp.repeat(jnp.arange(rows, dtype=F32), GRID_W)
    col = jnp.tile(jnp.arange(GRID_W, dtype=F32), rows)
    inv_freq = ROPE_BASE ** (-jnp.arange(0, ROPE_AXIS, 2, dtype=F32) / ROPE_AXIS)
    ang = jnp.stack([row, col], axis=-1)[:, :, None] * inv_freq
    cos = jnp.broadcast_to(jnp.cos(ang)[:, :, None, :], (s_, 2, 2, ROPE_AXIS // 2)).reshape(s_, MLA_ROPE)
    sin = jnp.sin(ang)[:, :, None, :] * jnp.array([-1.0, 1.0], F32)[None, None, :, None]
    sin = sin.reshape(s_, MLA_ROPE)
    ones = jnp.ones((s_, MLA_NOPE), F32)
    zeros = jnp.zeros((s_, LANES - MLA_NOPE - MLA_ROPE), F32)
    c_tab = jnp.concatenate([ones, cos, zeros], axis=1)
    s_tab = jnp.concatenate([jnp.zeros_like(ones), sin, zeros], axis=1)
    c_ctx = jnp.concatenate([jnp.ones((nc, MLA_NOPE + MLA_ROPE), F32),
                             jnp.zeros((nc, LANES - MLA_NOPE - MLA_ROPE), F32)], axis=1)
    return c_tab, s_tab, c_ctx, jnp.zeros_like(c_ctx)


def _dispatch(eidx, wts, b_, s_):
    n = TOP_K * s_
    e = eidx.reshape(TOP_K, b_, s_).transpose(1, 0, 2).reshape(b_, n)
    w = wts.reshape(TOP_K, b_, s_).transpose(1, 0, 2).reshape(b_, n)
    key = e * n + jnp.arange(n, dtype=jnp.int32)[None, :]
    key_sorted, w_sorted = lax.sort((key, w), dimension=1, num_keys=1)
    tok = ((key_sorted % n) % s_) * SUBLANES
    tok = jnp.pad(tok.astype(jnp.int32), ((0, 0), (0, EXPERT_TILE)))
    w_sorted = jnp.pad(w_sorted, ((0, 0), (0, EXPERT_TILE)))

    ids = jnp.arange(N_EXPERTS, dtype=jnp.int32)
    counts = jnp.sum(e[:, :, None] == ids[None, None, :], axis=1, dtype=jnp.int32)
    starts = jnp.cumsum(counts, axis=1) - counts
    tiles = (counts + EXPERT_TILE - 1) // EXPERT_TILE
    tile_end = jnp.cumsum(tiles, axis=1)
    ntiles = tile_end[:, -1:]
    tid = jnp.arange(n // EXPERT_TILE + N_EXPERTS + 3, dtype=jnp.int32)[None, :]
    live = tid < ntiles
    tile_e = jnp.sum(tid[:, :, None] >= tile_end[:, None, :], axis=-1, dtype=jnp.int32)
    last_e = jnp.max(jnp.where(tiles > 0, ids[None, :], 0), axis=1, keepdims=True)
    tile_e = jnp.where(live, tile_e, last_e)
    take = lambda a: jnp.take_along_axis(a, tile_e, axis=1)
    within = (tid - (take(tile_end) - take(tiles))) * EXPERT_TILE
    tile_src = jnp.where(live, take(starts) + within, 0)
    tile_n = jnp.where(live, jnp.clip(take(counts) - within, 0, EXPERT_TILE), 0)
    return (tile_e, tile_src, tile_n, ntiles), tok, w_sorted


def kernel(x, c, ctx, c_ctx, w_mod, b_mod, norm_mix_g, w_in, q_norm_g, w_q_up, kv_norm_g, w_kv_up, conv_w, conv_b,
           lru_w_a, lru_b_a, lru_w_x, lru_b_x, lru_lambda, w_out, norm_ffn_g, router_w, router_bias, exp_w_gate,
           exp_w_up, exp_w_down, sh_w_gate, sh_w_up, sh_w_down, final_norm_g):
    b_, s_, d = x.shape
    nc = ctx.shape[1]
    assert w_mod.shape[0] == 1, "single-layer operation"
    assert s_ % GRID_W == 0 and s_ % SCAN_CHUNK == 0 and nc % SCAN_CHUNK == 0
    assert s_ % min(TOKEN_TILE, s_) == 0 and s_ % min(Q_TILE, s_) == 0

    wts = _prep_weights(w_in[0], q_norm_g[0], w_q_up[0], kv_norm_g[0], w_kv_up[0], lru_w_a[0], lru_b_a[0],
                        lru_w_x[0], lru_b_x[0], w_out[0], router_w[0], router_bias[0], exp_w_gate[0], exp_w_up[0],
                        exp_w_down[0], sh_w_gate[0], sh_w_up[0], sh_w_down[0])
    c_tab, s_tab, c_ctx_tab, s_ctx_tab = _rope_tables(s_, nc)

    rows = (b_ + 1 + SUBLANES - 1) // SUBLANES * SUBLANES
    cc = jnp.zeros((rows, d), F32).at[:b_].set(c).at[b_].set(c_ctx)
    mod = _modulation(cc, w_mod[0], b_mod[0][None])
    sh1, sc1, g1, sh2, sc2, g2 = [mod[:b_, i * d:(i + 1) * d].reshape(b_, 1, d) for i in range(6)]
    csh1, csc1 = [mod[b_, i * d:(i + 1) * d].reshape(1, 1, d) for i in range(2)]

    q, k_lat, v_lat, lx_lat, gl = _input_projection(
        x, sh1, sc1, norm_mix_g, wts, (c_tab * ATTN_SCALE, s_tab * ATTN_SCALE, c_tab, s_tab), True)
    k_ctx, v_ctx, lx_ctx, _ = _input_projection(
        ctx, csh1, csc1, norm_mix_g, wts, (c_ctx_tab, s_ctx_tab, c_ctx_tab, s_ctx_tab), False)

    o_lat = _attention(q, k_lat, k_ctx, v_lat, v_ctx)
    y2 = _rglru(lx_lat, lx_ctx, gl, conv_w[0], conv_b[0][None], wts["wg"], wts["bg"],
                lru_lambda[0].reshape(2, 1, LRU_WIDTH))

    xs, fp, eidx, rw = _output_projection(o_lat, y2, x, g1, sh2, sc2, g2, norm_ffn_g, wts)

    tables, tok, wl = _dispatch(eidx, rw, b_, s_)
    buf = fp
    for b in range(b_):
        buf = _routed_experts([tb[b] for tb in tables], tok[b], wl[b], buf, wts["exp_gu"], wts["exp_d"], b)
    routed = buf.reshape(b_ * s_ * SUBLANES, LANES)

    return _final(xs, routed, g2, final_norm_g[None])
```

```python
import functools

import jax
import jax.numpy as jnp
from jax import lax
from jax.experimental import pallas as pl
from jax.experimental.pallas import tpu as pltpu

GRID_W = 64
EPS = 1e-6
MLA_HEADS = 8
MLA_NOPE = 64
MLA_ROPE = 32
MLA_V = 64
MLA_Q_RANK = 256
MLA_KV_RANK = 128
LRU_WIDTH = 512
LRU_BLOCKS = 8
CONV_W = 4
RG_C = 8.0
ROPE_BASE = 10000.0
ROPE_AXIS = MLA_ROPE // 2
ATTN_SCALE = (MLA_NOPE + MLA_ROPE) ** -0.5
N_EXPERTS = 64
TOP_K = 8
N_GROUPS = 8
TOPK_GROUPS = 4
EXPERT_FF = 256
ROUTED_SCALE = 2.5
COL_KV = MLA_Q_RANK
COL_KR = COL_KV + MLA_KV_RANK
COL_LRU_X = COL_KR + MLA_ROPE
COL_LRU_G = COL_LRU_X + LRU_WIDTH

LANES = 128
SUBLANES = 8
VMEM_LIMIT = 56 * 1024 * 1024

TOKEN_TILE = 512
OUTPROJ_SPLIT = 2
Q_TILE = 1024
KEY_CHUNK = 512
SCAN_CHUNK = 128
SCAN_GROUP = 4
EXPERT_TILE = 512
TILE_STRIDE = EXPERT_TILE + 1
SCATTER_UNROLL = 8

F32 = jnp.float32
BF16 = jnp.bfloat16
NEG_INF = float("-inf")


def _params(*sem):
    return pltpu.CompilerParams(dimension_semantics=sem, vmem_limit_bytes=VMEM_LIMIT)


def _rms(t, g):
    return t * lax.rsqrt(jnp.mean(t * t, axis=-1, keepdims=True) + EPS) * g


def _gelu_tanh(t):
    return 0.5 * t * (1.0 + jnp.tanh(0.7978845608028654 * (t + 0.044715 * (t * t * t))))


def _silu(t):
    return t * jax.nn.sigmoid(t)


def _rows_to_slabs(rows, stage_ref, slab_ref):
    n = rows.shape[0]
    st = n + 1
    for j in range(SUBLANES):
        stage_ref[j * st:j * st + n, :] = rows[:, j * LANES:(j + 1) * LANES]
    for r in range(n):
        slab_ref[r * SUBLANES:(r + 1) * SUBLANES, :] = stage_ref[pl.ds(r, SUBLANES, stride=st), :]


def _slabs_to_rows(slab_ref, stage_ref, n):
    st = n + 1
    for r in range(n):
        stage_ref[pl.ds(r, SUBLANES, stride=st), :] = slab_ref[r * SUBLANES:(r + 1) * SUBLANES, :]
    return jnp.concatenate([stage_ref[j * st:j * st + n, :] for j in range(SUBLANES)], axis=-1)


def _nt_dot(a, b):
    return lax.dot_general(a, b, (((1,), (1,)), ((), ())), preferred_element_type=F32)


def _mod_kernel(c_ref, w_ref, b_ref, o_ref):
    s = _silu(c_ref[...])
    o_ref[...] = jnp.dot(s, w_ref[...], precision=lax.Precision.HIGHEST,
                         preferred_element_type=F32) + b_ref[...]


def _modulation(cc, w, b):
    rows, d = cc.shape
    n = w.shape[1]
    tn = 1024
    return pl.pallas_call(
        _mod_kernel,
        out_shape=jax.ShapeDtypeStruct((rows, n), F32),
        grid=(n // tn,),
        in_specs=[pl.BlockSpec((rows, d), lambda j: (0, 0)),
                  pl.BlockSpec((d, tn), lambda j: (0, j)),
                  pl.BlockSpec((1, tn), lambda j: (0, j))],
        out_specs=pl.BlockSpec((rows, tn), lambda j: (0, j)),
        compiler_params=_params("arbitrary"),
        name="modulation",
    )(cc, w, b)


def _inproj_kernel(*refs, with_q):
    (x_ref, sh_ref, sc_ref, g_ref, win_ref, qg_ref, kvg_ref, wq_ref, wqs_ref, wk_ref, wks_ref,
     wv_ref, vo_ref, cq_ref, sq_ref, ck_ref, sk_ref) = refs[:17]
    if with_q:
        q_ref, k_ref, v_ref, lx_ref, lg_ref = refs[17:]
    else:
        k_ref, v_ref, lx_ref, lg_ref = refs[17:]
    h = _rms(x_ref[0], g_ref[...]) * (1.0 + sc_ref[0]) + sh_ref[0]
    p = jnp.dot(h.astype(BF16), win_ref[...], preferred_element_type=F32)

    if with_q:
        qn = _rms(p[:, :MLA_Q_RANK], qg_ref[...]).astype(BF16)
        qa = jnp.dot(qn, wq_ref[...], preferred_element_type=F32)
        qb = jnp.dot(qn, wqs_ref[...], preferred_element_type=F32)
        cq, sq = cq_ref[...], sq_ref[...]
        for hd in range(MLA_HEADS):
            sl = slice(hd * LANES, (hd + 1) * LANES)
            q_ref[0, :, sl] = (qa[:, sl] * cq + qb[:, sl] * sq).astype(BF16)

    kvn = _rms(p[:, COL_KV:COL_KR], kvg_ref[...])
    kvn16 = kvn.astype(BF16)
    kin = jnp.concatenate([kvn16, p[:, COL_KR:COL_KR + LANES].astype(BF16)], axis=-1)
    ka = jnp.dot(kin, wk_ref[...], preferred_element_type=F32)
    kb = jnp.dot(kin, wks_ref[...], preferred_element_type=F32)
    ck, sk = ck_ref[...], sk_ref[...]
    for hd in range(MLA_HEADS):
        sl = slice(hd * LANES, (hd + 1) * LANES)
        k_ref[0, :, sl] = (ka[:, sl] * ck + kb[:, sl] * sk).astype(BF16)
    v_ref[0] = (jnp.dot(kvn16, wv_ref[...], preferred_element_type=F32) + vo_ref[...]).astype(BF16)
    lx_ref[0] = p[:, 512:512 + LRU_WIDTH]
    lg_ref[0] = _gelu_tanh(p[:, 1024:1024 + LRU_WIDTH]).astype(BF16)


def _input_projection(x, sh, sc, g, wts, tabs, with_q):
    b_, s_, d = x.shape
    tm = min(TOKEN_TILE, s_)
    hw = MLA_HEADS * LANES
    per_batch = sh.shape[0] == b_
    mod_spec = pl.BlockSpec((1, 1, d), (lambda b, i: (b, 0, 0)) if per_batch else (lambda b, i: (0, 0, 0)))

    def const(a):
        return pl.BlockSpec(a.shape, lambda b, i: (0,) * a.ndim)

    tab_spec = pl.BlockSpec((tm, LANES), lambda b, i: (i, 0))
    wide = lambda w: pl.BlockSpec((1, tm, w), lambda b, i: (b, i, 0))
    weights = (wts["w_in"], wts["q_g"], wts["kv_g"], wts["wq"], wts["wq_sw"], wts["wk"], wts["wk_sw"], wts["wv"],
               wts["v_ones"])
    out_shape = [jax.ShapeDtypeStruct((b_, s_, hw), BF16),
                 jax.ShapeDtypeStruct((b_, s_, hw), BF16),
                 jax.ShapeDtypeStruct((b_, s_, LRU_WIDTH), F32),
                 jax.ShapeDtypeStruct((b_, s_, LRU_WIDTH), BF16)]
    out_specs = [wide(hw), wide(hw), wide(LRU_WIDTH), wide(LRU_WIDTH)]
    if with_q:
        out_shape = [jax.ShapeDtypeStruct((b_, s_, hw), BF16)] + out_shape
        out_specs = [wide(hw)] + out_specs
    return pl.pallas_call(
        functools.partial(_inproj_kernel, with_q=with_q),
        out_shape=out_shape,
        grid=(b_, s_ // tm),
        in_specs=[wide(d), mod_spec, mod_spec, const(g)] + [const(w) for w in weights] + [tab_spec] * 4,
        out_specs=out_specs,
        compiler_params=_params("parallel", "parallel"),
        name="input_projection_lat" if with_q else "input_projection_ctx",
    )(x, sh, sc, g, *weights, *tabs)


def _attn_kernel(q_ref, kl_ref, kc_ref, vl_ref, vc_ref, o_ref):
    s_ = kl_ref.shape[1]
    kc = min(KEY_CHUNK, s_)
    half = LANES // 2
    lane = lax.broadcasted_iota(jnp.int32, (q_ref.shape[1], LANES), 1)
    chunks = [(kl_ref, vl_ref, c * kc, kc) for c in range(s_ // kc)] + [(kc_ref, vc_ref, 0, kc_ref.shape[1])]
    items = [(a, ch) for a in range(2) for ch in chunks]

    def scores(item):
        a, (k_ref, _, start, size) = item
        sl = slice(a * LANES, (a + 1) * LANES)
        return _nt_dot(q_ref[0, :, sl], k_ref[0, start:start + size, sl])

    out = None
    s_next = scores(items[0])
    for i, (a, (_, v_ref, start, size)) in enumerate(items):
        s = s_next
        if i + 1 < len(items):
            s_next = scores(items[i + 1])
        first = i % len(chunks) == 0
        m_c = jnp.max(s, axis=-1, keepdims=True)
        m_new = m_c if first else jnp.maximum(m, m_c)
        p = jnp.exp(s - m_new).astype(BF16)
        pv = jnp.dot(p, v_ref[0, start:start + size, a * LANES:(a + 1) * LANES], preferred_element_type=F32)
        acc = pv if first else acc * jnp.exp(m - m_new) + pv
        m = m_new
        if (i + 1) % len(chunks) == 0:
            own = (lane < half) if a == 0 else (lane >= half)
            row_sum = pltpu.roll(acc, half, 1)
            o = jnp.where(own, acc / row_sum, 0.0)
            out = o if out is None else out + o
    o_ref[0] = out.astype(BF16)


def _attention(q, k_lat, k_ctx, v_lat, v_ctx):
    b_, s_, hw = q.shape
    nc = k_ctx.shape[1]
    tq = min(Q_TILE, s_)
    pair = 2 * LANES
    n_pairs = hw // pair
    return pl.pallas_call(
        _attn_kernel,
        out_shape=jax.ShapeDtypeStruct((b_, s_, n_pairs * LANES), BF16),
        grid=(b_, n_pairs, s_ // tq),
        in_specs=[pl.BlockSpec((1, tq, pair), lambda b, j, i: (b, i, j)),
                  pl.BlockSpec((1, s_, pair), lambda b, j, i: (b, 0, j)),
                  pl.BlockSpec((1, nc, pair), lambda b, j, i: (b, 0, j)),
                  pl.BlockSpec((1, s_, pair), lambda b, j, i: (b, 0, j)),
                  pl.BlockSpec((1, nc, pair), lambda b, j, i: (b, 0, j))],
        out_specs=pl.BlockSpec((1, tq, LANES), lambda b, j, i: (b, i, j)),
        compiler_params=_params("parallel", "parallel", "arbitrary"),
        name="attention",
    )(q, k_lat, k_ctx, v_lat, v_ctx)


def _scan_groups(a, b, reverse):
    a = a.reshape(a.shape[0] // SUBLANES, SUBLANES, a.shape[1])
    b = b.reshape(a.shape)
    row = lax.broadcasted_iota(jnp.int32, a.shape, 1)
    s = 1
    while s < SUBLANES:
        if reverse:
            keep = row < SUBLANES - s
            a_sh = jnp.where(keep, pltpu.roll(a, SUBLANES - s, 1), 1.0)
            b_sh = jnp.where(keep, pltpu.roll(b, SUBLANES - s, 1), 0.0)
        else:
            keep = row >= s
            a_sh = jnp.where(keep, pltpu.roll(a, s, 1), 1.0)
            b_sh = jnp.where(keep, pltpu.roll(b, s, 1), 0.0)
        b = a * b_sh + b
        a = a * a_sh
        s *= 2
    return a, b


def _chain_groups(scanned, carry, reverse):
    a, b = scanned
    n = a.shape[0]
    hs = [None] * n
    for v in (reversed(range(n)) if reverse else range(n)):
        hs[v] = a[v] * carry + b[v]
        carry = hs[v][0:1] if reverse else hs[v][SUBLANES - 1:SUBLANES]
    return jnp.concatenate(hs, axis=0), carry


def _lru_kernel(xl_ref, xc_ref, gl_ref, cw_ref, cb_ref, wg_ref, bg_ref, lam_ref, o_ref,
                padl_ref, padc_ref, hf_ref, hb_ref):
    s_ = xl_ref.shape[1]
    nc = xc_ref.shape[1]
    tc = SCAN_CHUNK
    halo = SUBLANES
    zero_halo = jnp.zeros((halo, LANES), F32)
    padl_ref[0:halo, :] = zero_halo
    padl_ref[halo:halo + s_, :] = xl_ref[0]
    padl_ref[halo + s_:2 * halo + s_, :] = zero_halo
    padc_ref[0:halo, :] = zero_halo
    padc_ref[halo:halo + nc, :] = xc_ref[0]
    padc_ref[halo + nc:2 * halo + nc, :] = zero_halo

    cw = cw_ref[...]
    cb = cb_ref[...]

    def coeffs(pad_ref, j, d):
        base = j * tc + halo - 2
        u = cb
        for k in range(CONV_W):
            u = u + cw[k:k + 1] * pad_ref[pl.ds(base + k, tc), :]
        gates = jnp.dot(u.astype(BF16), wg_ref[d, 0], preferred_element_type=F32) + bg_ref[d, 0]
        r = 0.5 * jnp.tanh(0.5 * gates[:, :LANES]) + 0.5
        i = 0.5 * jnp.tanh(0.5 * gates[:, LANES:]) + 0.5
        z = -lam_ref[d]
        softplus = jnp.maximum(z, 0.0) + jnp.log(1.0 + jnp.exp(-jnp.abs(z)))
        a = jnp.exp(((-RG_C) * softplus) * r)
        b = jnp.sqrt(1.0 - a * a) * (i * u)
        return a, b

    def local_scan(pad_ref, j, d):
        a, b = coeffs(pad_ref, j, d)
        return _scan_groups(a, b, reverse=(d == 1))

    def apply_carry(scanned, d, carry):
        return _chain_groups(scanned, carry, reverse=(d == 1))

    cf = cb_ = jnp.zeros((1, LANES), F32)
    n_ctx = nc // tc
    for j in range(n_ctx):
        _, cf = apply_carry(local_scan(padc_ref, j, 0), 0, cf)
        _, cb_ = apply_carry(local_scan(padc_ref, n_ctx - 1 - j, 1), 1, cb_)

    n_lat = s_ // tc
    grp = min(SCAN_GROUP, n_lat)

    def body(jj, carry):
        cf, cb_ = carry
        fwd = [jj * grp + g for g in range(grp)]
        bwd = [n_lat - 1 - jj * grp - g for g in range(grp)]
        scans_f = [local_scan(padl_ref, j, 0) for j in fwd]
        scans_b = [local_scan(padl_ref, j, 1) for j in bwd]
        for j, sc in zip(fwd, scans_f):
            h, cf = apply_carry(sc, 0, cf)
            hf_ref[pl.ds(pl.multiple_of(j * tc, tc), tc), :] = h
        for j, sc in zip(bwd, scans_b):
            h, cb_ = apply_carry(sc, 1, cb_)
            hb_ref[pl.ds(pl.multiple_of(j * tc, tc), tc), :] = h
        return cf, cb_

    lax.fori_loop(0, n_lat // grp, body, (cf, cb_))
    o_ref[0] = ((hf_ref[...] + hb_ref[...]) * gl_ref[0].astype(F32)).astype(BF16)


def _rglru(lx_lat, lx_ctx, gl, conv_w, conv_b, wg, bg, lam):
    b_, s_, w = lx_lat.shape
    nc = lx_ctx.shape[1]
    ng = w // LANES
    halo = SUBLANES
    return pl.pallas_call(
        _lru_kernel,
        out_shape=jax.ShapeDtypeStruct((b_, s_, w), BF16),
        grid=(b_, ng),
        in_specs=[pl.BlockSpec((1, s_, LANES), lambda b, g: (b, 0, g)),
                  pl.BlockSpec((1, nc, LANES), lambda b, g: (b, 0, g)),
                  pl.BlockSpec((1, s_, LANES), lambda b, g: (b, 0, g)),
                  pl.BlockSpec((CONV_W, LANES), lambda b, g: (0, g)),
                  pl.BlockSpec((1, LANES), lambda b, g: (0, g)),
                  pl.BlockSpec((2, 1, LANES, 2 * LANES), lambda b, g: (0, g, 0, 0)),
                  pl.BlockSpec((2, 1, 1, 2 * LANES), lambda b, g: (0, g, 0, 0)),
                  pl.BlockSpec((2, 1, LANES), lambda b, g: (0, 0, g))],
        out_specs=pl.BlockSpec((1, s_, LANES), lambda b, g: (b, 0, g)),
        scratch_shapes=[pltpu.VMEM((s_ + 2 * halo, LANES), F32),
                        pltpu.VMEM((nc + 2 * halo, LANES), F32),
                        pltpu.VMEM((s_, LANES), F32),
                        pltpu.VMEM((s_, LANES), F32)],
        compiler_params=_params("parallel", "parallel"),
        name="rglru",
    )(lx_lat, lx_ctx, gl, conv_w, conv_b, wg, bg, lam)


def _first_index(mask, iota, limit, axis):
    return jnp.min(jnp.where(mask, iota, limit), axis=axis, keepdims=True)


def _outproj_kernel(o_ref, y2_ref, x_ref, g1_ref, sh_ref, sc_ref, g2_ref, gf_ref, wo_ref, wrt_ref, rb_ref,
                    wsgu_ref, wsd_ref, xs_ref, fp_ref, e_ref, w_ref, stage_ref):
    half = wo_ref.shape[0] // 2
    hm = x_ref.shape[1] // OUTPROJ_SPLIT
    groups = [slice(h * hm, (h + 1) * hm) for h in range(OUTPROJ_SPLIT)]

    def mix_of(r):
        return (jnp.dot(o_ref[0, r, :], wo_ref[0:half, :], preferred_element_type=F32)
                + jnp.dot(y2_ref[0, r, :], wo_ref[half:, :], preferred_element_type=F32))

    def norm_of(r, mix):
        x1 = x_ref[0, r, :] + g1_ref[0] * mix
        f = _rms(x1, gf_ref[...]) * (1.0 + sc_ref[0]) + sh_ref[0]
        f16 = f.astype(BF16)
        return x1, f, f16, (f - f16.astype(F32)).astype(BF16)

    def project(f16, f_lo):
        both = _nt_dot(wrt_ref[...], f16)
        logits = both[:N_EXPERTS] + both[N_EXPERTS:] + _nt_dot(wrt_ref[0:N_EXPERTS, :], f_lo)
        return logits, jnp.dot(f16, wsgu_ref[...], preferred_element_type=F32)

    def finish(r, x1, f, a):
        ff = a.shape[1] // 2
        act = _silu(a[:, :ff]) * a[:, ff:]
        shared = jnp.dot(act.astype(BF16), wsd_ref[...], preferred_element_type=F32)
        xs_ref[0, r, :] = x1 + g2_ref[0] * shared
        _rows_to_slabs(f, stage_ref, fp_ref.at[0, pl.ds(r.start * SUBLANES, hm * SUBLANES)])

    mixes = [mix_of(r) for r in groups]
    normed = [None] * OUTPROJ_SPLIT
    projected = [None] * OUTPROJ_SPLIT
    normed[0] = norm_of(groups[0], mixes[0])
    for h in range(OUTPROJ_SPLIT):
        projected[h] = project(normed[h][2], normed[h][3])
        if h + 1 < OUTPROJ_SPLIT:
            normed[h + 1] = norm_of(groups[h + 1], mixes[h + 1])
    for h, r in enumerate(groups):
        _route(projected[h][0], rb_ref, e_ref, w_ref, r)
        finish(r, normed[h][0], normed[h][1], projected[h][1])


def _route(logits, rb_ref, e_ref, w_ref, cols):
    tm = logits.shape[1]
    scores = jax.nn.sigmoid(logits)
    sel = scores + rb_ref[...]
    per = N_EXPERTS // N_GROUPS
    g3 = sel.reshape(N_GROUPS, per, tm)
    mem = lax.broadcasted_iota(jnp.int32, g3.shape, 1)
    m1 = jnp.max(g3, axis=1, keepdims=True)
    first = _first_index(g3 == m1, mem, per, 1)
    m2 = jnp.max(jnp.where(mem == first, NEG_INF, g3), axis=1, keepdims=True)
    gscore = (m1 + m2).reshape(N_GROUPS, tm)
    giota = lax.broadcasted_iota(jnp.int32, gscore.shape, 0)
    gmask = jnp.zeros(gscore.shape, F32)
    cur = gscore
    for _ in range(TOPK_GROUPS):
        mx = jnp.max(cur, axis=0, keepdims=True)
        pick = giota == _first_index(cur == mx, giota, N_GROUPS, 0)
        gmask = jnp.where(pick, 1.0, gmask)
        cur = jnp.where(pick, NEG_INF, cur)
    allowed = jnp.broadcast_to(gmask.reshape(N_GROUPS, 1, tm), g3.shape) > 0.0
    cur = jnp.where(allowed, g3, NEG_INF).reshape(N_EXPERTS, tm)
    eiota = lax.broadcasted_iota(jnp.int32, cur.shape, 0)
    picked_w = []
    for k in range(TOP_K):
        mx = jnp.max(cur, axis=0, keepdims=True)
        idx = _first_index(cur == mx, eiota, N_EXPERTS, 0)
        pick = eiota == idx
        e_ref[k:k + 1, cols] = idx
        picked_w.append(jnp.sum(jnp.where(pick, scores, 0.0), axis=0, keepdims=True))
        cur = jnp.where(pick, NEG_INF, cur)
    wsum = picked_w[0]
    for k in range(1, TOP_K):
        wsum = wsum + picked_w[k]
    for k in range(TOP_K):
        w_ref[k:k + 1, cols] = ROUTED_SCALE * picked_w[k] / wsum


def _output_projection(o, y2, x, g1, sh2, sc2, g2, gf, wts):
    b_, s_, d = x.shape
    tm = min(TOKEN_TILE, s_)
    nt = s_ // tm
    mod_spec = pl.BlockSpec((1, 1, d), lambda b, i: (b, 0, 0))

    def const(a):
        return pl.BlockSpec(a.shape, lambda b, i: (0,) * a.ndim)

    wide = lambda w: pl.BlockSpec((1, tm, w), lambda b, i: (b, i, 0))
    route_spec = pl.BlockSpec((TOP_K, tm), lambda b, i: (0, b * nt + i))
    weights = (wts["w_out"], wts["router_t"], wts["router_b"], wts["sh_gu"], wts["sh_d"])
    return pl.pallas_call(
        _outproj_kernel,
        out_shape=[jax.ShapeDtypeStruct((b_, s_, d), F32),
                   jax.ShapeDtypeStruct((b_, s_ * SUBLANES, LANES), F32),
                   jax.ShapeDtypeStruct((TOP_K, b_ * s_), jnp.int32),
                   jax.ShapeDtypeStruct((TOP_K, b_ * s_), F32)],
        grid=(b_, nt),
        in_specs=[wide(o.shape[2]), wide(y2.shape[2]), wide(d), mod_spec, mod_spec, mod_spec, mod_spec,
                  const(gf)] + [const(w) for w in weights],
        out_specs=[wide(d), pl.BlockSpec((1, tm * SUBLANES, LANES), lambda b, i: (b, i, 0)), route_spec, route_spec],
        scratch_shapes=[pltpu.VMEM((SUBLANES * (tm + 1), LANES), F32)],
        compiler_params=_params("parallel", "parallel"),
        name="output_projection",
    )(o, y2, x, g1, sh2, sc2, g2, gf, *weights)


def _moe_kernel(tile_e_ref, tile_src_ref, tile_n_ref, ntiles_ref, tok_ref, wl_ref,
                fp_ref, wgu_ref, wd_ref, acc_ref, tin_a, tin_b, tout_a, tout_b, act_a, act_b):
    t = pl.program_id(0)
    m = EXPERT_TILE
    st = TILE_STRIDE
    n_out = tout_a.shape[0] // st

    def slab(off):
        return pl.ds(pl.multiple_of(off, SUBLANES), SUBLANES)

    def gather(tile, tin):
        base = tile_src_ref[tile]
        for mi in range(m):
            tin[pl.ds(mi, SUBLANES, stride=st), :] = fp_ref[slab(tok_ref[base + mi]), :]

    def up_project(tin, act):
        xt = jnp.concatenate([tin[j * st:j * st + m, :].astype(BF16) for j in range(n_out)], axis=-1)
        hcat = jnp.dot(xt, wgu_ref[0], preferred_element_type=F32)
        ff = hcat.shape[1] // 2
        act[...] = (_silu(hcat[:, :ff]) * hcat[:, ff:]).astype(BF16)

    def down_project(tile, act, tout):
        y = jnp.dot(act[...], wd_ref[0], preferred_element_type=F32)
        valid = lax.broadcasted_iota(jnp.int32, (m, LANES), 0) < tile_n_ref[tile]
        for j in range(n_out):
            tout[j * st:j * st + m, :] = jnp.where(valid, y[:, j * LANES:(j + 1) * LANES], 0.0)

    def scatter(tile, tout):
        base = tile_src_ref[tile]
        for g in range(m // SCATTER_UNROLL):
            pend = []
            for r in range(SCATTER_UNROLL):
                row = g * SCATTER_UNROLL + r
                dst = slab(tok_ref[base + row])
                pend.append((dst, acc_ref[dst, :] + wl_ref[base + row] * tout[pl.ds(row, SUBLANES, stride=st), :]))
            for dst, new in reversed(pend):
                acc_ref[dst, :] = new

    @pl.when(t == 0)
    def _():
        acc_ref[...] = jnp.zeros(acc_ref.shape, F32)
        tout_a[...] = jnp.zeros(tout_a.shape, F32)
        act_b[...] = jnp.zeros(act_b.shape, BF16)
        gather(0, tin_a)

    def step(tin_cur, tin_nxt, act_cur, act_prev, tout_cur, tout_prev):
        gather(t + 1, tin_nxt)
        up_project(tin_cur, act_cur)
        down_project(jnp.maximum(t - 1, 0), act_prev, tout_prev)
        scatter(jnp.maximum(t - 2, 0), tout_cur)

    @pl.when(t % 2 == 0)
    def _():
        step(tin_a, tin_b, act_a, act_b, tout_a, tout_b)

    @pl.when(t % 2 == 1)
    def _():
        step(tin_b, tin_a, act_b, act_a, tout_b, tout_a)


def _routed_experts(tables, tok, wl, buf, wgu, wd, b):
    d = wd.shape[2]
    _, rows, _ = buf.shape
    assert d == SUBLANES * LANES
    stage = d // LANES * TILE_STRIDE
    block = pl.BlockSpec((None, rows, LANES), lambda t, *_: (b, 0, 0), pipeline_mode=pl.Buffered(1))
    n_prefetch = len(tables) + 2
    grid_spec = pltpu.PrefetchScalarGridSpec(
        num_scalar_prefetch=n_prefetch,
        grid=(tables[3][0] + 2,),
        in_specs=[block,
                  pl.BlockSpec((1,) + wgu.shape[1:], lambda t, te, *_: (te[t], 0, 0)),
                  pl.BlockSpec((1,) + wd.shape[1:], lambda t, te, *_: (te[jnp.maximum(t - 1, 0)], 0, 0))],
        out_specs=block,
        scratch_shapes=[pltpu.VMEM((stage, LANES), F32) for _ in range(4)]
        + [pltpu.VMEM((EXPERT_TILE, wd.shape[1]), BF16) for _ in range(2)],
    )
    return pl.pallas_call(
        _moe_kernel,
        out_shape=jax.ShapeDtypeStruct(buf.shape, F32),
        grid_spec=grid_spec,
        input_output_aliases={n_prefetch: 0},
        compiler_params=_params("arbitrary"),
        name="routed_experts",
    )(*tables, tok, wl, buf, wgu, wd)


def _final_kernel(xs_ref, r_ref, g2_ref, g_ref, o_ref, stage_ref):
    routed = _slabs_to_rows(r_ref, stage_ref, xs_ref.shape[1])
    o_ref[0] = _rms(xs_ref[0] + g2_ref[0] * routed, g_ref[...])


def _final(xs, routed, g2, g):
    b_, s_, d = xs.shape
    tm = min(TOKEN_TILE, s_)
    nt = s_ // tm
    wide = pl.BlockSpec((1, tm, d), lambda b, i: (b, i, 0))
    return pl.pallas_call(
        _final_kernel,
        out_shape=jax.ShapeDtypeStruct((b_, s_, d), F32),
        grid=(b_, nt),
        in_specs=[wide, pl.BlockSpec((tm * SUBLANES, LANES), lambda b, i: (b * nt + i, 0)),
                  pl.BlockSpec((1, 1, d), lambda b, i: (b, 0, 0)),
                  pl.BlockSpec((1, d), lambda b, i: (0, 0))],
        out_specs=wide,
        scratch_shapes=[pltpu.VMEM((SUBLANES * (tm + 1), LANES), F32)],
        compiler_params=_params("parallel", "parallel"),
        name="final_norm",
    )(xs, routed, g2, g)


def _split_bf16(w):
    hi = w.astype(BF16)
    lo = (w - hi.astype(F32)).astype(BF16)
    return jnp.concatenate([hi, lo], axis=0)


def _prep_weights(w_in, q_norm_g, w_q_up, kv_norm_g, w_kv_up, lru_w_a, lru_b_a, lru_w_x, lru_b_x, w_out,
                  router_w, router_bias, exp_w_gate, exp_w_up, exp_w_down, sh_w_gate, sh_w_up, sh_w_down):
    d = w_in.shape[0]
    h_ = MLA_HEADS
    pad_kr = jnp.zeros((d, LANES - MLA_ROPE), F32)
    w_in_p = jnp.concatenate([w_in[:, :COL_LRU_X], pad_kr, w_in[:, COL_LRU_X:]], axis=1).astype(BF16)

    wq = w_q_up.reshape(MLA_Q_RANK, h_, MLA_NOPE + MLA_ROPE)
    nope, rope = wq[:, :, :MLA_NOPE], wq[:, :, MLA_NOPE:]
    rope_sw = rope.reshape(MLA_Q_RANK, h_, 2, 2, ROPE_AXIS // 2)[:, :, :, ::-1, :].reshape(rope.shape)
    zpad = jnp.zeros((MLA_Q_RANK, h_, LANES - MLA_NOPE - MLA_ROPE), F32)
    wq_p = jnp.concatenate([nope, rope, zpad], axis=-1).reshape(MLA_Q_RANK, h_ * LANES).astype(BF16)
    wq_sw = jnp.concatenate([jnp.zeros_like(nope), rope_sw, zpad], axis=-1).reshape(MLA_Q_RANK, h_ * LANES).astype(BF16)

    wkv = w_kv_up.reshape(MLA_KV_RANK, h_, MLA_NOPE + MLA_V)
    k_nope, v_w = wkv[:, :, :MLA_NOPE], wkv[:, :, MLA_NOPE:]
    r_idx = jnp.arange(MLA_ROPE)
    place = jnp.zeros((LANES, h_, LANES), F32).at[r_idx, :, MLA_NOPE + r_idx].set(1.0)
    place_sw = jnp.zeros((LANES, h_, LANES), F32).at[r_idx ^ (ROPE_AXIS // 2), :, MLA_NOPE + r_idx].set(1.0)
    k_top = jnp.concatenate([k_nope, jnp.zeros((MLA_KV_RANK, h_, LANES - MLA_NOPE), F32)], axis=-1)
    wk = jnp.concatenate([k_top, place], axis=0).reshape(MLA_KV_RANK + LANES, h_ * LANES).astype(BF16)
    wk_sw = jnp.concatenate([jnp.zeros_like(k_top), place_sw], axis=0).reshape(MLA_KV_RANK + LANES, h_ * LANES).astype(BF16)
    zv = jnp.zeros_like(v_w)
    even = (jnp.arange(h_) % 2 == 0)[None, :, None]
    wv = jnp.concatenate([jnp.where(even, v_w, zv), jnp.where(even, zv, v_w)], axis=-1)
    wv = wv.reshape(MLA_KV_RANK, h_ * LANES).astype(BF16)
    one_lo = jnp.concatenate([jnp.zeros((MLA_V,), F32), jnp.ones((LANES - MLA_V,), F32)])
    one_hi = jnp.concatenate([jnp.ones((LANES - MLA_V,), F32), jnp.zeros((MLA_V,), F32)])
    v_ones = jnp.where(even[0], one_lo[None, :], one_hi[None, :]).reshape(1, h_ * LANES)

    eye = jnp.eye(LRU_BLOCKS, dtype=F32)
    ng = LRU_WIDTH // LANES

    def dense(w):
        return jnp.einsum("xncd,nm->xncmd", w, eye).reshape(2, LRU_WIDTH, LRU_WIDTH)

    def grp(wd_):
        return jnp.stack([wd_[:, g * LANES:(g + 1) * LANES, g * LANES:(g + 1) * LANES] for g in range(ng)], axis=1)

    wg = jnp.concatenate([grp(dense(lru_w_a)), grp(dense(lru_w_x))], axis=-1).astype(BF16)
    bg = jnp.concatenate([lru_b_a.reshape(2, ng, 1, LANES), lru_b_x.reshape(2, ng, 1, LANES)], axis=-1)

    return dict(
        w_in=w_in_p, q_g=q_norm_g[None], kv_g=kv_norm_g[None], wq=wq_p, wq_sw=wq_sw, wk=wk, wk_sw=wk_sw, wv=wv,
        v_ones=v_ones,
        wg=wg, bg=bg,
        w_out=w_out.astype(BF16), router_t=_split_bf16(router_w.T), router_b=router_bias[:, None],
        sh_gu=jnp.concatenate([sh_w_gate, sh_w_up], axis=1).astype(BF16), sh_d=sh_w_down.astype(BF16),
        exp_gu=jnp.concatenate([exp_w_gate, exp_w_up], axis=2).astype(BF16), exp_d=exp_w_down.astype(BF16),
    )


def _rope_tables(s_, nc):
    rows = s_ // GRID_W
    row = jnp.repeat(jnp.arange(rows, dtype=F32), GRID_W)
    col = jnp.tile(jnp.arange(GRID_W, dtype=F32), rows)
    inv_freq = ROPE_BASE ** (-jnp.arange(0, ROPE_AXIS, 2, dtype=F32) / ROPE_AXIS)
    ang = jnp.stack([row, col], axis=-1)[:, :, None] * inv_freq
    cos = jnp.broadcast_to(jnp.cos(ang)[:, :, None, :], (s_, 2, 2, ROPE_AXIS // 2)).reshape(s_, MLA_ROPE)
    sin = jnp.sin(ang)[:, :, None, :] * jnp.array([-1.0, 1.0], F32)[None, None, :, None]
    sin = sin.reshape(s_, MLA_ROPE)
    ones = jnp.ones((s_, MLA_NOPE), F32)
    zeros = jnp.zeros((s_, LANES - MLA_NOPE - MLA_ROPE), F32)
    c_tab = jnp.concatenate([ones, cos, zeros], axis=1)
    s_tab = jnp.concatenate([jnp.zeros_like(ones), sin, zeros], axis=1)
    c_ctx = jnp.concatenate([jnp.ones((nc, MLA_NOPE + MLA_ROPE), F32),
                             jnp.zeros((nc, LANES - MLA_NOPE - MLA_ROPE), F32)], axis=1)
    return c_tab, s_tab, c_ctx, jnp.zeros_like(c_ctx)


def _dispatch(eidx, wts, b_, s_):
    n = TOP_K * s_
    e = eidx.reshape(TOP_K, b_, s_).transpose(1, 0, 2).reshape(b_, n)
    w = wts.reshape(TOP_K, b_, s_).transpose(1, 0, 2).reshape(b_, n)
    key = e * n + jnp.arange(n, dtype=jnp.int32)[None, :]
    key_sorted, w_sorted = lax.sort((key, w), dimension=1, num_keys=1)
    tok = ((key_sorted % n) % s_) * SUBLANES
    tok = jnp.pad(tok.astype(jnp.int32), ((0, 0), (0, EXPERT_TILE)))
    w_sorted = jnp.pad(w_sorted, ((0, 0), (0, EXPERT_TILE)))

    ids = jnp.arange(N_EXPERTS, dtype=jnp.int32)
    counts = jnp.sum(e[:, :, None] == ids[None, None, :], axis=1, dtype=jnp.int32)
    starts = jnp.cumsum(counts, axis=1) - counts
    tiles = (counts + EXPERT_TILE - 1) // EXPERT_TILE
    tile_end = jnp.cumsum(tiles, axis=1)
    ntiles = tile_end[:, -1:]
    tid = jnp.arange(n // EXPERT_TILE + N_EXPERTS + 3, dtype=jnp.int32)[None, :]
    live = tid < ntiles
    tile_e = jnp.sum(tid[:, :, None] >= tile_end[:, None, :], axis=-1, dtype=jnp.int32)
    last_e = jnp.max(jnp.where(tiles > 0, ids[None, :], 0), axis=1, keepdims=True)
    tile_e = jnp.where(live, tile_e, last_e)
    take = lambda a: jnp.take_along_axis(a, tile_e, axis=1)
    within = (tid - (take(tile_end) - take(tiles))) * EXPERT_TILE
    tile_src = jnp.where(live, take(starts) + within, 0)
    tile_n = jnp.where(live, jnp.clip(take(counts) - within, 0, EXPERT_TILE), 0)
    return (tile_e, tile_src, tile_n, ntiles), tok, w_sorted


def kernel(x, c, ctx, c_ctx, w_mod, b_mod, norm_mix_g, w_in, q_norm_g, w_q_up, kv_norm_g, w_kv_up, conv_w, conv_b,
           lru_w_a, lru_b_a, lru_w_x, lru_b_x, lru_lambda, w_out, norm_ffn_g, router_w, router_bias, exp_w_gate,
           exp_w_up, exp_w_down, sh_w_gate, sh_w_up, sh_w_down, final_norm_g):
    b_, s_, d = x.shape
    nc = ctx.shape[1]
    assert w_mod.shape[0] == 1, "single-layer operation"
    assert s_ % GRID_W == 0 and s_ % SCAN_CHUNK == 0 and nc % SCAN_CHUNK == 0
    assert s_ % min(TOKEN_TILE, s_) == 0 and s_ % min(Q_TILE, s_) == 0

    wts = _prep_weights(w_in[0], q_norm_g[0], w_q_up[0], kv_norm_g[0], w_kv_up[0], lru_w_a[0], lru_b_a[0],
                        lru_w_x[0], lru_b_x[0], w_out[0], router_w[0], router_bias[0], exp_w_gate[0], exp_w_up[0],
                        exp_w_down[0], sh_w_gate[0], sh_w_up[0], sh_w_down[0])
    c_tab, s_tab, c_ctx_tab, s_ctx_tab = _rope_tables(s_, nc)

    rows = (b_ + 1 + SUBLANES - 1) // SUBLANES * SUBLANES
    cc = jnp.zeros((rows, d), F32).at[:b_].set(c).at[b_].set(c_ctx)
    mod = _modulation(cc, w_mod[0], b_mod[0][None])
    sh1, sc1, g1, sh2, sc2, g2 = [mod[:b_, i * d:(i + 1) * d].reshape(b_, 1, d) for i in range(6)]
    csh1, csc1 = [mod[b_, i * d:(i + 1) * d].reshape(1, 1, d) for i in range(2)]

    q, k_lat, v_lat, lx_lat, gl = _input_projection(
        x, sh1, sc1, norm_mix_g, wts, (c_tab * ATTN_SCALE, s_tab * ATTN_SCALE, c_tab, s_tab), True)
    k_ctx, v_ctx, lx_ctx, _ = _input_projection(
        ctx, csh1, csc1, norm_mix_g, wts, (c_ctx_tab, s_ctx_tab, c_ctx_tab, s_ctx_tab), False)

    o_lat = _attention(q, k_lat, k_ctx, v_lat, v_ctx)
    y2 = _rglru(lx_lat, lx_ctx, gl, conv_w[0], conv_b[0][None], wts["wg"], wts["bg"],
                lru_lambda[0].reshape(2, 1, LRU_WIDTH))

    xs, fp, eidx, rw = _output_projection(o_lat, y2, x, g1, sh2, sc2, g2, norm_ffn_g, wts)

    tables, tok, wl = _dispatch(eidx, rw, b_, s_)
    buf = fp
    for b in range(b_):
        buf = _routed_experts([tb[b] for tb in tables], tok[b], wl[b], buf, wts["exp_gu"], wts["exp_d"], b)
    routed = buf.reshape(b_ * s_ * SUBLANES, LANES)

    return _final(xs, routed, g2, final_norm_g[None])
```

```python
import functools

import jax
import jax.numpy as jnp
from jax import lax
from jax.experimental import pallas as pl
from jax.experimental.pallas import tpu as pltpu

GRID_W = 64
EPS = 1e-6
MLA_HEADS = 8
MLA_NOPE = 64
MLA_ROPE = 32
MLA_V = 64
MLA_Q_RANK = 256
MLA_KV_RANK = 128
LRU_WIDTH = 512
LRU_BLOCKS = 8
CONV_W = 4
RG_C = 8.0
ROPE_BASE = 10000.0
ROPE_AXIS = MLA_ROPE // 2
ATTN_SCALE = (MLA_NOPE + MLA_ROPE) ** -0.5
N_EXPERTS = 64
TOP_K = 8
N_GROUPS = 8
TOPK_GROUPS = 4
EXPERT_FF = 256
ROUTED_SCALE = 2.5
COL_KV = MLA_Q_RANK
COL_KR = COL_KV + MLA_KV_RANK
COL_LRU_X = COL_KR + MLA_ROPE
COL_LRU_G = COL_LRU_X + LRU_WIDTH

LANES = 128
SUBLANES = 8
VMEM_LIMIT = 56 * 1024 * 1024

TOKEN_TILE = 512
OUTPROJ_SPLIT = 2
Q_TILE = 1024
KEY_CHUNK = 512
SCAN_CHUNK = 128
SCAN_GROUP = 4
EXPERT_TILE = 512
TILE_STRIDE = EXPERT_TILE + 1
SCATTER_UNROLL = 8

F32 = jnp.float32
BF16 = jnp.bfloat16
NEG_INF = float("-inf")


def _params(*sem):
    return pltpu.CompilerParams(dimension_semantics=sem, vmem_limit_bytes=VMEM_LIMIT)


def _rms(t, g):
    return t * lax.rsqrt(jnp.mean(t * t, axis=-1, keepdims=True) + EPS) * g


def _gelu_tanh(t):
    return 0.5 * t * (1.0 + jnp.tanh(0.7978845608028654 * (t + 0.044715 * (t * t * t))))


def _silu(t):
    return t * jax.nn.sigmoid(t)


def _rows_to_slabs(rows, stage_ref, slab_ref):
    n = rows.shape[0]
    st = n + 1
    for j in range(SUBLANES):
        stage_ref[j * st:j * st + n, :] = rows[:, j * LANES:(j + 1) * LANES]
    for r in range(n):
        slab_ref[r * SUBLANES:(r + 1) * SUBLANES, :] = stage_ref[pl.ds(r, SUBLANES, stride=st), :]


def _slabs_to_rows(slab_ref, stage_ref, n):
    st = n + 1
    for r in range(n):
        stage_ref[pl.ds(r, SUBLANES, stride=st), :] = slab_ref[r * SUBLANES:(r + 1) * SUBLANES, :]
    return jnp.concatenate([stage_ref[j * st:j * st + n, :] for j in range(SUBLANES)], axis=-1)


def _nt_dot(a, b):
    return lax.dot_general(a, b, (((1,), (1,)), ((), ())), preferred_element_type=F32)


def _mod_kernel(c_ref, w_ref, b_ref, o_ref):
    s = _silu(c_ref[...])
    o_ref[...] = jnp.dot(s, w_ref[...], precision=lax.Precision.HIGHEST,
                         preferred_element_type=F32) + b_ref[...]


def _modulation(cc, w, b):
    rows, d = cc.shape
    n = w.shape[1]
    tn = 1024
    return pl.pallas_call(
        _mod_kernel,
        out_shape=jax.ShapeDtypeStruct((rows, n), F32),
        grid=(n // tn,),
        in_specs=[pl.BlockSpec((rows, d), lambda j: (0, 0)),
                  pl.BlockSpec((d, tn), lambda j: (0, j)),
                  pl.BlockSpec((1, tn), lambda j: (0, j))],
        out_specs=pl.BlockSpec((rows, tn), lambda j: (0, j)),
        compiler_params=_params("arbitrary"),
        name="modulation",
    )(cc, w, b)


def _inproj_kernel(*refs, with_q):
    (x_ref, sh_ref, sc_ref, g_ref, win_ref, qg_ref, kvg_ref, wq_ref, wqs_ref, wk_ref, wks_ref,
     wv_ref, vo_ref, cq_ref, sq_ref, ck_ref, sk_ref) = refs[:17]
    if with_q:
        q_ref, k_ref, v_ref, lx_ref, lg_ref = refs[17:]
    else:
        k_ref, v_ref, lx_ref, lg_ref = refs[17:]
    h = _rms(x_ref[0], g_ref[...]) * (1.0 + sc_ref[0]) + sh_ref[0]
    p = jnp.dot(h.astype(BF16), win_ref[...], preferred_element_type=F32)

    if with_q:
        qn = _rms(p[:, :MLA_Q_RANK], qg_ref[...]).astype(BF16)
        qa = jnp.dot(qn, wq_ref[...], preferred_element_type=F32)
        qb = jnp.dot(qn, wqs_ref[...], preferred_element_type=F32)
        cq, sq = cq_ref[...], sq_ref[...]
        for hd in range(MLA_HEADS):
            sl = slice(hd * LANES, (hd + 1) * LANES)
            q_ref[0, :, sl] = (qa[:, sl] * cq + qb[:, sl] * sq).astype(BF16)

    kvn = _rms(p[:, COL_KV:COL_KR], kvg_ref[...])
    kvn16 = kvn.astype(BF16)
    kin = jnp.concatenate([kvn16, p[:, COL_KR:COL_KR + LANES].astype(BF16)], axis=-1)
    ka = jnp.dot(kin, wk_ref[...], preferred_element_type=F32)
    kb = jnp.dot(kin, wks_ref[...], preferred_element_type=F32)
    ck, sk = ck_ref[...], sk_ref[...]
    for hd in range(MLA_HEADS):
        sl = slice(hd * LANES, (hd + 1) * LANES)
        k_ref[0, :, sl] = (ka[:, sl] * ck + kb[:, sl] * sk).astype(BF16)
    v_ref[0] = (jnp.dot(kvn16, wv_ref[...], preferred_element_type=F32) + vo_ref[...]).astype(BF16)
    lx_ref[0] = p[:, 512:512 + LRU_WIDTH]
    lg_ref[0] = _gelu_tanh(p[:, 1024:1024 + LRU_WIDTH]).astype(BF16)


def _input_projection(x, sh, sc, g, wts, tabs, with_q):
    b_, s_, d = x.shape
    tm = min(TOKEN_TILE, s_)
    hw = MLA_HEADS * LANES
    per_batch = sh.shape[0] == b_
    mod_spec = pl.BlockSpec((1, 1, d), (lambda b, i: (b, 0, 0)) if per_batch else (lambda b, i: (0, 0, 0)))

    def const(a):
        return pl.BlockSpec(a.shape, lambda b, i: (0,) * a.ndim)

    tab_spec = pl.BlockSpec((tm, LANES), lambda b, i: (i, 0))
    wide = lambda w: pl.BlockSpec((1, tm, w), lambda b, i: (b, i, 0))
    weights = (wts["w_in"], wts["q_g"], wts["kv_g"], wts["wq"], wts["wq_sw"], wts["wk"], wts["wk_sw"], wts["wv"],
               wts["v_ones"])
    out_shape = [jax.ShapeDtypeStruct((b_, s_, hw), BF16),
                 jax.ShapeDtypeStruct((b_, s_, hw), BF16),
                 jax.ShapeDtypeStruct((b_, s_, LRU_WIDTH), F32),
                 jax.ShapeDtypeStruct((b_, s_, LRU_WIDTH), BF16)]
    out_specs = [wide(hw), wide(hw), wide(LRU_WIDTH), wide(LRU_WIDTH)]
    if with_q:
        out_shape = [jax.ShapeDtypeStruct((b_, s_, hw), BF16)] + out_shape
        out_specs = [wide(hw)] + out_specs
    return pl.pallas_call(
        functools.partial(_inproj_kernel, with_q=with_q),
        out_shape=out_shape,
        grid=(b_, s_ // tm),
        in_specs=[wide(d), mod_spec, mod_spec, const(g)] + [const(w) for w in weights] + [tab_spec] * 4,
        out_specs=out_specs,
        compiler_params=_params("parallel", "parallel"),
        name="input_projection_lat" if with_q else "input_projection_ctx",
    )(x, sh, sc, g, *weights, *tabs)


def _attn_kernel(q_ref, kl_ref, kc_ref, vl_ref, vc_ref, o_ref):
    s_ = kl_ref.shape[1]
    kc = min(KEY_CHUNK, s_)
    half = LANES // 2
    lane = lax.broadcasted_iota(jnp.int32, (q_ref.shape[1], LANES), 1)
    chunks = [(kl_ref, vl_ref, c * kc, kc) for c in range(s_ // kc)] + [(kc_ref, vc_ref, 0, kc_ref.shape[1])]
    items = [(a, ch) for a in range(2) for ch in chunks]

    def scores(item):
        a, (k_ref, _, start, size) = item
        sl = slice(a * LANES, (a + 1) * LANES)
        return _nt_dot(q_ref[0, :, sl], k_ref[0, start:start + size, sl])

    out = None
    s_next = scores(items[0])
    for i, (a, (_, v_ref, start, size)) in enumerate(items):
        s = s_next
        if i + 1 < len(items):
            s_next = scores(items[i + 1])
        first = i % len(chunks) == 0
        m_c = jnp.max(s, axis=-1, keepdims=True)
        m_new = m_c if first else jnp.maximum(m, m_c)
        p = jnp.exp(s - m_new).astype(BF16)
        pv = jnp.dot(p, v_ref[0, start:start + size, a * LANES:(a + 1) * LANES], preferred_element_type=F32)
        acc = pv if first else acc * jnp.exp(m - m_new) + pv
        m = m_new
        if (i + 1) % len(chunks) == 0:
            own = (lane < half) if a == 0 else (lane >= half)
            row_sum = pltpu.roll(acc, half, 1)
            o = jnp.where(own, acc / row_sum, 0.0)
            out = o if out is None else out + o
    o_ref[0] = out.astype(BF16)


def _attention(q, k_lat, k_ctx, v_lat, v_ctx):
    b_, s_, hw = q.shape
    nc = k_ctx.shape[1]
    tq = min(Q_TILE, s_)
    pair = 2 * LANES
    n_pairs = hw // pair
    return pl.pallas_call(
        _attn_kernel,
        out_shape=jax.ShapeDtypeStruct((b_, s_, n_pairs * LANES), BF16),
        grid=(b_, n_pairs, s_ // tq),
        in_specs=[pl.BlockSpec((1, tq, pair), lambda b, j, i: (b, i, j)),
                  pl.BlockSpec((1, s_, pair), lambda b, j, i: (b, 0, j)),
                  pl.BlockSpec((1, nc, pair), lambda b, j, i: (b, 0, j)),
                  pl.BlockSpec((1, s_, pair), lambda b, j, i: (b, 0, j)),
                  pl.BlockSpec((1, nc, pair), lambda b, j, i: (b, 0, j))],
        out_specs=pl.BlockSpec((1, tq, LANES), lambda b, j, i: (b, i, j)),
        compiler_params=_params("parallel", "parallel", "arbitrary"),
        name="attention",
    )(q, k_lat, k_ctx, v_lat, v_ctx)


def _scan_groups(a, b, reverse):
    a = a.reshape(a.shape[0] // SUBLANES, SUBLANES, a.shape[1])
    b = b.reshape(a.shape)
    row = lax.broadcasted_iota(jnp.int32, a.shape, 1)
    s = 1
    while s < SUBLANES:
        if reverse:
            keep = row < SUBLANES - s
            a_sh = jnp.where(keep, pltpu.roll(a, SUBLANES - s, 1), 1.0)
            b_sh = jnp.where(keep, pltpu.roll(b, SUBLANES - s, 1), 0.0)
        else:
            keep = row >= s
            a_sh = jnp.where(keep, pltpu.roll(a, s, 1), 1.0)
            b_sh = jnp.where(keep, pltpu.roll(b, s, 1), 0.0)
        b = a * b_sh + b
        a = a * a_sh
        s *= 2
    return a, b


def _chain_groups(scanned, carry, reverse):
    a, b = scanned
    n = a.shape[0]
    hs = [None] * n
    for v in (reversed(range(n)) if reverse else range(n)):
        hs[v] = a[v] * carry + b[v]
        carry = hs[v][0:1] if reverse else hs[v][SUBLANES - 1:SUBLANES]
    return jnp.concatenate(hs, axis=0), carry


def _lru_kernel(xl_ref, xc_ref, gl_ref, cw_ref, cb_ref, wg_ref, bg_ref, lam_ref, o_ref,
                padl_ref, padc_ref, hf_ref, hb_ref):
    s_ = xl_ref.shape[1]
    nc = xc_ref.shape[1]
    tc = SCAN_CHUNK
    halo = SUBLANES
    zero_halo = jnp.zeros((halo, LANES), F32)
    padl_ref[0:halo, :] = zero_halo
    padl_ref[halo:halo + s_, :] = xl_ref[0]
    padl_ref[halo + s_:2 * halo + s_, :] = zero_halo
    padc_ref[0:halo, :] = zero_halo
    padc_ref[halo:halo + nc, :] = xc_ref[0]
    padc_ref[halo + nc:2 * halo + nc, :] = zero_halo

    cw = cw_ref[...]
    cb = cb_ref[...]

    def coeffs(pad_ref, j, d):
        base = j * tc + halo - 2
        u = cb
        for k in range(CONV_W):
            u = u + cw[k:k + 1] * pad_ref[pl.ds(base + k, tc), :]
        t = jnp.tanh(jnp.dot(u.astype(BF16), wg_ref[d, 0], preferred_element_type=F32) + bg_ref[d, 0])
        z = -lam_ref[d]
        softplus = jnp.maximum(z, 0.0) + jnp.log(1.0 + jnp.exp(-jnp.abs(z)))
        half_rate = (-0.5 * RG_C) * softplus
        a = jnp.exp(half_rate * t[:, :LANES] + half_rate)
        b = jnp.sqrt(1.0 - a * a) * ((0.5 * t[:, LANES:] + 0.5) * u)
        return a, b

    def local_scan(pad_ref, j, d):
        a, b = coeffs(pad_ref, j, d)
        return _scan_groups(a, b, reverse=(d == 1))

    def apply_carry(scanned, d, carry):
        return _chain_groups(scanned, carry, reverse=(d == 1))

    cf = cb_ = jnp.zeros((1, LANES), F32)
    n_ctx = nc // tc
    for j in range(n_ctx):
        _, cf = apply_carry(local_scan(padc_ref, j, 0), 0, cf)
        _, cb_ = apply_carry(local_scan(padc_ref, n_ctx - 1 - j, 1), 1, cb_)

    n_lat = s_ // tc
    grp = min(SCAN_GROUP, n_lat)

    def body(jj, carry):
        cf, cb_ = carry
        fwd = [jj * grp + g for g in range(grp)]
        bwd = [n_lat - 1 - jj * grp - g for g in range(grp)]
        scans_f = [local_scan(padl_ref, j, 0) for j in fwd]
        scans_b = [local_scan(padl_ref, j, 1) for j in bwd]
        for j, sc in zip(fwd, scans_f):
            h, cf = apply_carry(sc, 0, cf)
            hf_ref[pl.ds(pl.multiple_of(j * tc, tc), tc), :] = h
        for j, sc in zip(bwd, scans_b):
            h, cb_ = apply_carry(sc, 1, cb_)
            hb_ref[pl.ds(pl.multiple_of(j * tc, tc), tc), :] = h
        return cf, cb_

    lax.fori_loop(0, n_lat // grp, body, (cf, cb_))
    o_ref[0] = ((hf_ref[...] + hb_ref[...]) * gl_ref[0].astype(F32)).astype(BF16)


def _rglru(lx_lat, lx_ctx, gl, conv_w, conv_b, wg, bg, lam):
    b_, s_, w = lx_lat.shape
    nc = lx_ctx.shape[1]
    ng = w // LANES
    halo = SUBLANES
    return pl.pallas_call(
        _lru_kernel,
        out_shape=jax.ShapeDtypeStruct((b_, s_, w), BF16),
        grid=(b_, ng),
        in_specs=[pl.BlockSpec((1, s_, LANES), lambda b, g: (b, 0, g)),
                  pl.BlockSpec((1, nc, LANES), lambda b, g: (b, 0, g)),
                  pl.BlockSpec((1, s_, LANES), lambda b, g: (b, 0, g)),
                  pl.BlockSpec((CONV_W, LANES), lambda b, g: (0, g)),
                  pl.BlockSpec((1, LANES), lambda b, g: (0, g)),
                  pl.BlockSpec((2, 1, LANES, 2 * LANES), lambda b, g: (0, g, 0, 0)),
                  pl.BlockSpec((2, 1, 1, 2 * LANES), lambda b, g: (0, g, 0, 0)),
                  pl.BlockSpec((2, 1, LANES), lambda b, g: (0, 0, g))],
        out_specs=pl.BlockSpec((1, s_, LANES), lambda b, g: (b, 0, g)),
        scratch_shapes=[pltpu.VMEM((s_ + 2 * halo, LANES), F32),
                        pltpu.VMEM((nc + 2 * halo, LANES), F32),
                        pltpu.VMEM((s_, LANES), F32),
                        pltpu.VMEM((s_, LANES), F32)],
        compiler_params=_params("parallel", "parallel"),
        name="rglru",
    )(lx_lat, lx_ctx, gl, conv_w, conv_b, wg, bg, lam)


def _first_index(mask, iota, limit, axis):
    return jnp.min(jnp.where(mask, iota, limit), axis=axis, keepdims=True)


def _outproj_kernel(o_ref, y2_ref, x_ref, g1_ref, sh_ref, sc_ref, g2_ref, gf_ref, wo_ref, wrt_ref, rb_ref,
                    wsgu_ref, wsd_ref, xs_ref, fp_ref, e_ref, w_ref, stage_ref):
    half = wo_ref.shape[0] // 2
    hm = x_ref.shape[1] // OUTPROJ_SPLIT
    groups = [slice(h * hm, (h + 1) * hm) for h in range(OUTPROJ_SPLIT)]

    def mix_of(r):
        return (jnp.dot(o_ref[0, r, :], wo_ref[0:half, :], preferred_element_type=F32)
                + jnp.dot(y2_ref[0, r, :], wo_ref[half:, :], preferred_element_type=F32))

    def norm_of(r, mix):
        x1 = x_ref[0, r, :] + g1_ref[0] * mix
        f = _rms(x1, gf_ref[...]) * (1.0 + sc_ref[0]) + sh_ref[0]
        f16 = f.astype(BF16)
        return x1, f, f16, (f - f16.astype(F32)).astype(BF16)

    def project(f16, f_lo):
        both = _nt_dot(wrt_ref[...], f16)
        logits = both[:N_EXPERTS] + both[N_EXPERTS:] + _nt_dot(wrt_ref[0:N_EXPERTS, :], f_lo)
        return logits, jnp.dot(f16, wsgu_ref[...], preferred_element_type=F32)

    def finish(r, x1, f, a):
        ff = a.shape[1] // 2
        act = _silu(a[:, :ff]) * a[:, ff:]
        shared = jnp.dot(act.astype(BF16), wsd_ref[...], preferred_element_type=F32)
        xs_ref[0, r, :] = x1 + g2_ref[0] * shared
        _rows_to_slabs(f, stage_ref, fp_ref.at[0, pl.ds(r.start * SUBLANES, hm * SUBLANES)])

    mixes = [mix_of(r) for r in groups]
    normed = [None] * OUTPROJ_SPLIT
    projected = [None] * OUTPROJ_SPLIT
    normed[0] = norm_of(groups[0], mixes[0])
    for h in range(OUTPROJ_SPLIT):
        projected[h] = project(normed[h][2], normed[h][3])
        if h + 1 < OUTPROJ_SPLIT:
            normed[h + 1] = norm_of(groups[h + 1], mixes[h + 1])
    for h, r in enumerate(groups):
        _route(projected[h][0], rb_ref, e_ref, w_ref, r)
        finish(r, normed[h][0], normed[h][1], projected[h][1])


def _route(logits, rb_ref, e_ref, w_ref, cols):
    tm = logits.shape[1]
    scores = jax.nn.sigmoid(logits)
    sel = scores + rb_ref[...]
    per = N_EXPERTS // N_GROUPS
    g3 = sel.reshape(N_GROUPS, per, tm)
    mem = lax.broadcasted_iota(jnp.int32, g3.shape, 1)
    m1 = jnp.max(g3, axis=1, keepdims=True)
    first = _first_index(g3 == m1, mem, per, 1)
    m2 = jnp.max(jnp.where(mem == first, NEG_INF, g3), axis=1, keepdims=True)
    gscore = (m1 + m2).reshape(N_GROUPS, tm)
    giota = lax.broadcasted_iota(jnp.int32, gscore.shape, 0)
    gmask = jnp.zeros(gscore.shape, F32)
    cur = gscore
    for _ in range(TOPK_GROUPS):
        mx = jnp.max(cur, axis=0, keepdims=True)
        pick = giota == _first_index(cur == mx, giota, N_GROUPS, 0)
        gmask = jnp.where(pick, 1.0, gmask)
        cur = jnp.where(pick, NEG_INF, cur)
    allowed = jnp.broadcast_to(gmask.reshape(N_GROUPS, 1, tm), g3.shape) > 0.0
    cur = jnp.where(allowed, g3, NEG_INF).reshape(N_EXPERTS, tm)
    eiota = lax.broadcasted_iota(jnp.int32, cur.shape, 0)
    picked_w = []
    for k in range(TOP_K):
        mx = jnp.max(cur, axis=0, keepdims=True)
        idx = _first_index(cur == mx, eiota, N_EXPERTS, 0)
        pick = eiota == idx
        e_ref[k:k + 1, cols] = idx
        picked_w.append(jnp.sum(jnp.where(pick, scores, 0.0), axis=0, keepdims=True))
        cur = jnp.where(pick, NEG_INF, cur)
    wsum = picked_w[0]
    for k in range(1, TOP_K):
        wsum = wsum + picked_w[k]
    for k in range(TOP_K):
        w_ref[k:k + 1, cols] = ROUTED_SCALE * picked_w[k] / wsum


def _output_projection(o, y2, x, g1, sh2, sc2, g2, gf, wts):
    b_, s_, d = x.shape
    tm = min(TOKEN_TILE, s_)
    nt = s_ // tm
    mod_spec = pl.BlockSpec((1, 1, d), lambda b, i: (b, 0, 0))

    def const(a):
        return pl.BlockSpec(a.shape, lambda b, i: (0,) * a.ndim)

    wide = lambda w: pl.BlockSpec((1, tm, w), lambda b, i: (b, i, 0))
    route_spec = pl.BlockSpec((TOP_K, tm), lambda b, i: (0, b * nt + i))
    weights = (wts["w_out"], wts["router_t"], wts["router_b"], wts["sh_gu"], wts["sh_d"])
    return pl.pallas_call(
        _outproj_kernel,
        out_shape=[jax.ShapeDtypeStruct((b_, s_, d), F32),
                   jax.ShapeDtypeStruct((b_, s_ * SUBLANES, LANES), F32),
                   jax.ShapeDtypeStruct((TOP_K, b_ * s_), jnp.int32),
                   jax.ShapeDtypeStruct((TOP_K, b_ * s_), F32)],
        grid=(b_, nt),
        in_specs=[wide(o.shape[2]), wide(y2.shape[2]), wide(d), mod_spec, mod_spec, mod_spec, mod_spec,
                  const(gf)] + [const(w) for w in weights],
        out_specs=[wide(d), pl.BlockSpec((1, tm * SUBLANES, LANES), lambda b, i: (b, i, 0)), route_spec, route_spec],
        scratch_shapes=[pltpu.VMEM((SUBLANES * (tm + 1), LANES), F32)],
        compiler_params=_params("parallel", "parallel"),
        name="output_projection",
    )(o, y2, x, g1, sh2, sc2, g2, gf, *weights)


def _moe_kernel(tile_e_ref, tile_src_ref, tile_n_ref, ntiles_ref, tok_ref, wl_ref,
                fp_ref, wgu_ref, wd_ref, acc_ref, tin_a, tin_b, tout_a, tout_b, act_a, act_b):
    t = pl.program_id(0)
    m = EXPERT_TILE
    st = TILE_STRIDE
    n_out = tout_a.shape[0] // st

    def slab(off):
        return pl.ds(pl.multiple_of(off, SUBLANES), SUBLANES)

    def gather(tile, tin):
        base = tile_src_ref[tile]
        for mi in range(m):
            tin[pl.ds(mi, SUBLANES, stride=st), :] = fp_ref[slab(tok_ref[base + mi]), :]

    def up_project(tin, act):
        xt = jnp.concatenate([tin[j * st:j * st + m, :].astype(BF16) for j in range(n_out)], axis=-1)
        hcat = jnp.dot(xt, wgu_ref[0], preferred_element_type=F32)
        ff = hcat.shape[1] // 2
        act[...] = (_silu(hcat[:, :ff]) * hcat[:, ff:]).astype(BF16)

    def down_project(tile, act, tout):
        y = jnp.dot(act[...], wd_ref[0], preferred_element_type=F32)
        valid = lax.broadcasted_iota(jnp.int32, (m, LANES), 0) < tile_n_ref[tile]
        for j in range(n_out):
            tout[j * st:j * st + m, :] = jnp.where(valid, y[:, j * LANES:(j + 1) * LANES], 0.0)

    def scatter(tile, tout):
        base = tile_src_ref[tile]
        for g in range(m // SCATTER_UNROLL):
            pend = []
            for r in range(SCATTER_UNROLL):
                row = g * SCATTER_UNROLL + r
                dst = slab(tok_ref[base + row])
                pend.append((dst, acc_ref[dst, :] + wl_ref[base + row] * tout[pl.ds(row, SUBLANES, stride=st), :]))
            for dst, new in reversed(pend):
                acc_ref[dst, :] = new

    @pl.when(t == 0)
    def _():
        acc_ref[...] = jnp.zeros(acc_ref.shape, F32)
        tout_a[...] = jnp.zeros(tout_a.shape, F32)
        act_b[...] = jnp.zeros(act_b.shape, BF16)
        gather(0, tin_a)

    def step(tin_cur, tin_nxt, act_cur, act_prev, tout_cur, tout_prev):
        gather(t + 1, tin_nxt)
        up_project(tin_cur, act_cur)
        down_project(jnp.maximum(t - 1, 0), act_prev, tout_prev)
        scatter(jnp.maximum(t - 2, 0), tout_cur)

    active = t <= ntiles_ref[0] + 1

    @pl.when(active & (t % 2 == 0))
    def _():
        step(tin_a, tin_b, act_a, act_b, tout_a, tout_b)

    @pl.when(active & (t % 2 == 1))
    def _():
        step(tin_b, tin_a, act_b, act_a, tout_b, tout_a)


def _routed_experts(tables, tok, wl, buf, wgu, wd, b):
    d = wd.shape[2]
    _, rows, _ = buf.shape
    assert d == SUBLANES * LANES
    stage = d // LANES * TILE_STRIDE
    block = pl.BlockSpec((None, rows, LANES), lambda t, *_: (b, 0, 0), pipeline_mode=pl.Buffered(1))
    n_prefetch = len(tables) + 2
    grid_spec = pltpu.PrefetchScalarGridSpec(
        num_scalar_prefetch=n_prefetch,
        grid=(tables[0].shape[0] - 1,),
        in_specs=[block,
                  pl.BlockSpec((1,) + wgu.shape[1:], lambda t, te, *_: (te[t], 0, 0)),
                  pl.BlockSpec((1,) + wd.shape[1:], lambda t, te, *_: (te[jnp.maximum(t - 1, 0)], 0, 0))],
        out_specs=block,
        scratch_shapes=[pltpu.VMEM((stage, LANES), F32) for _ in range(4)]
        + [pltpu.VMEM((EXPERT_TILE, wd.shape[1]), BF16) for _ in range(2)],
    )
    return pl.pallas_call(
        _moe_kernel,
        out_shape=jax.ShapeDtypeStruct(buf.shape, F32),
        grid_spec=grid_spec,
        input_output_aliases={n_prefetch: 0},
        compiler_params=_params("arbitrary"),
        name="routed_experts",
    )(*tables, tok, wl, buf, wgu, wd)


def _final_kernel(xs_ref, r_ref, g2_ref, g_ref, o_ref, stage_ref):
    routed = _slabs_to_rows(r_ref, stage_ref, xs_ref.shape[1])
    o_ref[0] = _rms(xs_ref[0] + g2_ref[0] * routed, g_ref[...])


def _final(xs, routed, g2, g):
    b_, s_, d = xs.shape
    tm = min(TOKEN_TILE, s_)
    nt = s_ // tm
    wide = pl.BlockSpec((1, tm, d), lambda b, i: (b, i, 0))
    return pl.pallas_call(
        _final_kernel,
        out_shape=jax.ShapeDtypeStruct((b_, s_, d), F32),
        grid=(b_, nt),
        in_specs=[wide, pl.BlockSpec((tm * SUBLANES, LANES), lambda b, i: (b * nt + i, 0)),
                  pl.BlockSpec((1, 1, d), lambda b, i: (b, 0, 0)),
                  pl.BlockSpec((1, d), lambda b, i: (0, 0))],
        out_specs=wide,
        scratch_shapes=[pltpu.VMEM((SUBLANES * (tm + 1), LANES), F32)],
        compiler_params=_params("parallel", "parallel"),
        name="final_norm",
    )(xs, routed, g2, g)


def _split_bf16(w):
    hi = w.astype(BF16)
    lo = (w - hi.astype(F32)).astype(BF16)
    return jnp.concatenate([hi, lo], axis=0)


def _prep_weights(w_in, q_norm_g, w_q_up, kv_norm_g, w_kv_up, lru_w_a, lru_b_a, lru_w_x, lru_b_x, w_out,
                  router_w, router_bias, exp_w_gate, exp_w_up, exp_w_down, sh_w_gate, sh_w_up, sh_w_down):
    d = w_in.shape[0]
    h_ = MLA_HEADS
    pad_kr = jnp.zeros((d, LANES - MLA_ROPE), F32)
    w_in_p = jnp.concatenate([w_in[:, :COL_LRU_X], pad_kr, w_in[:, COL_LRU_X:]], axis=1).astype(BF16)

    wq = w_q_up.reshape(MLA_Q_RANK, h_, MLA_NOPE + MLA_ROPE)
    nope, rope = wq[:, :, :MLA_NOPE], wq[:, :, MLA_NOPE:]
    rope_sw = rope.reshape(MLA_Q_RANK, h_, 2, 2, ROPE_AXIS // 2)[:, :, :, ::-1, :].reshape(rope.shape)
    zpad = jnp.zeros((MLA_Q_RANK, h_, LANES - MLA_NOPE - MLA_ROPE), F32)
    wq_p = jnp.concatenate([nope, rope, zpad], axis=-1).reshape(MLA_Q_RANK, h_ * LANES).astype(BF16)
    wq_sw = jnp.concatenate([jnp.zeros_like(nope), rope_sw, zpad], axis=-1).reshape(MLA_Q_RANK, h_ * LANES).astype(BF16)

    wkv = w_kv_up.reshape(MLA_KV_RANK, h_, MLA_NOPE + MLA_V)
    k_nope, v_w = wkv[:, :, :MLA_NOPE], wkv[:, :, MLA_NOPE:]
    r_idx = jnp.arange(MLA_ROPE)
    place = jnp.zeros((LANES, h_, LANES), F32).at[r_idx, :, MLA_NOPE + r_idx].set(1.0)
    place_sw = jnp.zeros((LANES, h_, LANES), F32).at[r_idx ^ (ROPE_AXIS // 2), :, MLA_NOPE + r_idx].set(1.0)
    k_top = jnp.concatenate([k_nope, jnp.zeros((MLA_KV_RANK, h_, LANES - MLA_NOPE), F32)], axis=-1)
    wk = jnp.concatenate([k_top, place], axis=0).reshape(MLA_KV_RANK + LANES, h_ * LANES).astype(BF16)
    wk_sw = jnp.concatenate([jnp.zeros_like(k_top), place_sw], axis=0).reshape(MLA_KV_RANK + LANES, h_ * LANES).astype(BF16)
    zv = jnp.zeros_like(v_w)
    even = (jnp.arange(h_) % 2 == 0)[None, :, None]
    wv = jnp.concatenate([jnp.where(even, v_w, zv), jnp.where(even, zv, v_w)], axis=-1)
    wv = wv.reshape(MLA_KV_RANK, h_ * LANES).astype(BF16)
    one_lo = jnp.concatenate([jnp.zeros((MLA_V,), F32), jnp.ones((LANES - MLA_V,), F32)])
    one_hi = jnp.concatenate([jnp.ones((LANES - MLA_V,), F32), jnp.zeros((MLA_V,), F32)])
    v_ones = jnp.where(even[0], one_lo[None, :], one_hi[None, :]).reshape(1, h_ * LANES)

    eye = jnp.eye(LRU_BLOCKS, dtype=F32)
    ng = LRU_WIDTH // LANES

    def dense(w):
        return jnp.einsum("xncd,nm->xncmd", w, eye).reshape(2, LRU_WIDTH, LRU_WIDTH)

    def grp(wd_):
        return jnp.stack([wd_[:, g * LANES:(g + 1) * LANES, g * LANES:(g + 1) * LANES] for g in range(ng)], axis=1)

    wg = (0.5 * jnp.concatenate([grp(dense(lru_w_a)), grp(dense(lru_w_x))], axis=-1)).astype(BF16)
    bg = 0.5 * jnp.concatenate([lru_b_a.reshape(2, ng, 1, LANES), lru_b_x.reshape(2, ng, 1, LANES)], axis=-1)

    return dict(
        w_in=w_in_p, q_g=q_norm_g[None], kv_g=kv_norm_g[None], wq=wq_p, wq_sw=wq_sw, wk=wk, wk_sw=wk_sw, wv=wv,
        v_ones=v_ones,
        wg=wg, bg=bg,
        w_out=w_out.astype(BF16), router_t=_split_bf16(router_w.T), router_b=router_bias[:, None],
        sh_gu=jnp.concatenate([sh_w_gate, sh_w_up], axis=1).astype(BF16), sh_d=sh_w_down.astype(BF16),
        exp_gu=jnp.concatenate([exp_w_gate, exp_w_up], axis=2).astype(BF16), exp_d=exp_w_down.astype(BF16),
    )


def _rope_tables(s_, nc):
    rows = s_ // GRID_W
    row = jnp.repeat(jnp.arange(rows, dtype=F32), GRID_W)
    col = jnp.tile(jnp.arange(GRID_W, dtype=F32), rows)
    inv_freq = ROPE_BASE ** (-jnp.arange(0, ROPE_AXIS, 2, dtype=F32) / ROPE_AXIS)
    ang = jnp.stack([row, col], axis=-1)[:, :, None] * inv_freq
    cos = jnp.broadcast_to(jnp.cos(ang)[:, :, None, :], (s_, 2, 2, ROPE_AXIS // 2)).reshape(s_, MLA_ROPE)
    sin = jnp.sin(ang)[:, :, None, :] * jnp.array([-1.0, 1.0], F32)[None, None, :, None]
    sin = sin.reshape(s_, MLA_ROPE)
    ones = jnp.ones((s_, MLA_NOPE), F32)
    zeros = jnp.zeros((s_, LANES - MLA_NOPE - MLA_ROPE), F32)
    c_tab = jnp.concatenate([ones, cos, zeros], axis=1)
    s_tab = jnp.concatenate([jnp.zeros_like(ones), sin, zeros], axis=1)
    c_ctx = jnp.concatenate([jnp.ones((nc, MLA_NOPE + MLA_ROPE), F32),
                             jnp.zeros((nc, LANES - MLA_NOPE - MLA_ROPE), F32)], axis=1)
    return c_tab, s_tab, c_ctx, jnp.zeros_like(c_ctx)


def _dispatch(eidx, wts, b_, s_):
    n = TOP_K * s_
    e = eidx.reshape(TOP_K, b_, s_).transpose(1, 0, 2).reshape(b_, n)
    w = wts.reshape(TOP_K, b_, s_).transpose(1, 0, 2).reshape(b_, n)
    key = e * n + jnp.arange(n, dtype=jnp.int32)[None, :]
    key_sorted, w_sorted = lax.sort((key, w), dimension=1, num_keys=1)
    tok = ((key_sorted % n) % s_) * SUBLANES
    tok = jnp.pad(tok.astype(jnp.int32), ((0, 0), (0, EXPERT_TILE)))
    w_sorted = jnp.pad(w_sorted, ((0, 0), (0, EXPERT_TILE)))

    ids = jnp.arange(N_EXPERTS, dtype=jnp.int32)
    counts = jnp.sum(e[:, :, None] == ids[None, None, :], axis=1, dtype=jnp.int32)
    starts = jnp.cumsum(counts, axis=1) - counts
    tiles = (counts + EXPERT_TILE - 1) // EXPERT_TILE
    tile_end = jnp.cumsum(tiles, axis=1)
    ntiles = tile_end[:, -1:]
    tid = jnp.arange(n // EXPERT_TILE + N_EXPERTS + 3, dtype=jnp.int32)[None, :]
    live = tid < ntiles
    tile_e = jnp.sum(tid[:, :, None] >= tile_end[:, None, :], axis=-1, dtype=jnp.int32)
    last_e = jnp.max(jnp.where(tiles > 0, ids[None, :], 0), axis=1, keepdims=True)
    tile_e = jnp.where(live, tile_e, last_e)
    take = lambda a: jnp.take_along_axis(a, tile_e, axis=1)
    within = (tid - (take(tile_end) - take(tiles))) * EXPERT_TILE
    tile_src = jnp.where(live, take(starts) + within, 0)
    tile_n = jnp.where(live, jnp.clip(take(counts) - within, 0, EXPERT_TILE), 0)
    return (tile_e, tile_src, tile_n, ntiles), tok, w_sorted


def kernel(x, c, ctx, c_ctx, w_mod, b_mod, norm_mix_g, w_in, q_norm_g, w_q_up, kv_norm_g, w_kv_up, conv_w, conv_b,
           lru_w_a, lru_b_a, lru_w_x, lru_b_x, lru_lambda, w_out, norm_ffn_g, router_w, router_bias, exp_w_gate,
           exp_w_up, exp_w_down, sh_w_gate, sh_w_up, sh_w_down, final_norm_g):
    b_, s_, d = x.shape
    nc = ctx.shape[1]
    assert w_mod.shape[0] == 1, "single-layer operation"
    assert s_ % GRID_W == 0 and s_ % SCAN_CHUNK == 0 and nc % SCAN_CHUNK == 0
    assert s_ % min(TOKEN_TILE, s_) == 0 and s_ % min(Q_TILE, s_) == 0

    wts = _prep_weights(w_in[0], q_norm_g[0], w_q_up[0], kv_norm_g[0], w_kv_up[0], lru_w_a[0], lru_b_a[0],
                        lru_w_x[0], lru_b_x[0], w_out[0], router_w[0], router_bias[0], exp_w_gate[0], exp_w_up[0],
                        exp_w_down[0], sh_w_gate[0], sh_w_up[0], sh_w_down[0])
    c_tab, s_tab, c_ctx_tab, s_ctx_tab = _rope_tables(s_, nc)

    rows = (b_ + 1 + SUBLANES - 1) // SUBLANES * SUBLANES
    cc = jnp.zeros((rows, d), F32).at[:b_].set(c).at[b_].set(c_ctx)
    mod = _modulation(cc, w_mod[0], b_mod[0][None])
    sh1, sc1, g1, sh2, sc2, g2 = [mod[:b_, i * d:(i + 1) * d].reshape(b_, 1, d) for i in range(6)]
    csh1, csc1 = [mod[b_, i * d:(i + 1) * d].reshape(1, 1, d) for i in range(2)]

    q, k_lat, v_lat, lx_lat, gl = _input_projection(
        x, sh1, sc1, norm_mix_g, wts, (c_tab * ATTN_SCALE, s_tab * ATTN_SCALE, c_tab, s_tab), True)
    k_ctx, v_ctx, lx_ctx, _ = _input_projection(
        ctx, csh1, csc1, norm_mix_g, wts, (c_ctx_tab, s_ctx_tab, c_ctx_tab, s_ctx_tab), False)

    o_lat = _attention(q, k_lat, k_ctx, v_lat, v_ctx)
    y2 = _rglru(lx_lat, lx_ctx, gl, conv_w[0], conv_b[0][None], wts["wg"], wts["bg"],
                lru_lambda[0].reshape(2, 1, LRU_WIDTH))

    xs, fp, eidx, rw = _output_projection(o_lat, y2, x, g1, sh2, sc2, g2, norm_ffn_g, wts)

    tables, tok, wl = _dispatch(eidx, rw, b_, s_)
    buf = fp
    for b in range(b_):
        buf = _routed_experts([tb[b] for tb in tables], tok[b], wl[b], buf, wts["exp_gu"], wts["exp_d"], b)
    routed = buf.reshape(b_ * s_ * SUBLANES, LANES)

    return _final(xs, routed, g2, final_norm_g[None])
```

```python
import functools

import jax
import jax.numpy as jnp
from jax import lax
from jax.experimental import pallas as pl
from jax.experimental.pallas import tpu as pltpu

GRID_W = 64
EPS = 1e-6
MLA_HEADS = 8
MLA_NOPE = 64
MLA_ROPE = 32
MLA_V = 64
MLA_Q_RANK = 256
MLA_KV_RANK = 128
LRU_WIDTH = 512
LRU_BLOCKS = 8
CONV_W = 4
RG_C = 8.0
ROPE_BASE = 10000.0
ROPE_AXIS = MLA_ROPE // 2
ATTN_SCALE = (MLA_NOPE + MLA_ROPE) ** -0.5
N_EXPERTS = 64
TOP_K = 8
N_GROUPS = 8
TOPK_GROUPS = 4
EXPERT_FF = 256
ROUTED_SCALE = 2.5
COL_KV = MLA_Q_RANK
COL_KR = COL_KV + MLA_KV_RANK
COL_LRU_X = COL_KR + MLA_ROPE
COL_LRU_G = COL_LRU_X + LRU_WIDTH

LANES = 128
SUBLANES = 8
VMEM_LIMIT = 56 * 1024 * 1024

TOKEN_TILE = 512
FINAL_TILE = 1024
OUTPROJ_SPLIT = 2
Q_TILE = 1024
KEY_CHUNK = 512
SCAN_CHUNK = 128
SCAN_GROUP = 32
EXPERT_TILE = 512
TILE_STRIDE = EXPERT_TILE + 1
SCATTER_UNROLL = 8

F32 = jnp.float32
BF16 = jnp.bfloat16
NEG_INF = float("-inf")


def _params(*sem):
    return pltpu.CompilerParams(dimension_semantics=sem, vmem_limit_bytes=VMEM_LIMIT)


def _rms(t, g):
    return t * lax.rsqrt(jnp.mean(t * t, axis=-1, keepdims=True) + EPS) * g


def _gelu_tanh(t):
    return 0.5 * t * (1.0 + jnp.tanh(0.7978845608028654 * (t + 0.044715 * (t * t * t))))


def _silu(t):
    return t * jax.nn.sigmoid(t)


def _rows_to_slabs(rows, stage_ref, slab_ref):
    n = rows.shape[0]
    st = n + 1
    for j in range(SUBLANES):
        stage_ref[j * st:j * st + n, :] = rows[:, j * LANES:(j + 1) * LANES]
    for r in range(n):
        slab_ref[r * SUBLANES:(r + 1) * SUBLANES, :] = stage_ref[pl.ds(r, SUBLANES, stride=st), :]


def _slabs_to_rows(slab_ref, stage_ref, n):
    st = n + 1
    for r in range(n):
        stage_ref[pl.ds(r, SUBLANES, stride=st), :] = slab_ref[r * SUBLANES:(r + 1) * SUBLANES, :]
    return jnp.concatenate([stage_ref[j * st:j * st + n, :] for j in range(SUBLANES)], axis=-1)


def _nt_dot(a, b):
    return lax.dot_general(a, b, (((1,), (1,)), ((), ())), preferred_element_type=F32)


def _mod_kernel(c_ref, w_ref, b_ref, o_ref):
    s = _silu(c_ref[...])
    o_ref[...] = jnp.dot(s, w_ref[...], precision=lax.Precision.HIGHEST,
                         preferred_element_type=F32) + b_ref[...]


def _modulation(cc, w, b):
    rows, d = cc.shape
    n = w.shape[1]
    tn = 1024
    return pl.pallas_call(
        _mod_kernel,
        out_shape=jax.ShapeDtypeStruct((rows, n), F32),
        grid=(n // tn,),
        in_specs=[pl.BlockSpec((rows, d), lambda j: (0, 0)),
                  pl.BlockSpec((d, tn), lambda j: (0, j)),
                  pl.BlockSpec((1, tn), lambda j: (0, j))],
        out_specs=pl.BlockSpec((rows, tn), lambda j: (0, j)),
        compiler_params=_params("arbitrary"),
        name="modulation",
    )(cc, w, b)


def _inproj_kernel(*refs, with_q):
    (x_ref, sh_ref, sc_ref, g_ref, win_ref, qg_ref, kvg_ref, wq_ref, wqs_ref, wk_ref, wks_ref,
     wv_ref, vo_ref, cq_ref, sq_ref, ck_ref, sk_ref) = refs[:17]
    if with_q:
        q_ref, k_ref, v_ref, lx_ref, lg_ref = refs[17:]
    else:
        k_ref, v_ref, lx_ref, lg_ref = refs[17:]
    h = _rms(x_ref[0], g_ref[...]) * (1.0 + sc_ref[0]) + sh_ref[0]
    p = jnp.dot(h.astype(BF16), win_ref[...], preferred_element_type=F32)

    if with_q:
        qn = _rms(p[:, :MLA_Q_RANK], qg_ref[...]).astype(BF16)
        qa = jnp.dot(qn, wq_ref[...], preferred_element_type=F32)
        qb = jnp.dot(qn, wqs_ref[...], preferred_element_type=F32)
        cq, sq = cq_ref[...], sq_ref[...]
        for hd in range(MLA_HEADS):
            sl = slice(hd * LANES, (hd + 1) * LANES)
            q_ref[0, :, sl] = (qa[:, sl] * cq + qb[:, sl] * sq).astype(BF16)

    kvn = _rms(p[:, COL_KV:COL_KR], kvg_ref[...])
    kvn16 = kvn.astype(BF16)
    kin = jnp.concatenate([kvn16, p[:, COL_KR:COL_KR + LANES].astype(BF16)], axis=-1)
    ka = jnp.dot(kin, wk_ref[...], preferred_element_type=F32)
    kb = jnp.dot(kin, wks_ref[...], preferred_element_type=F32)
    ck, sk = ck_ref[...], sk_ref[...]
    for hd in range(MLA_HEADS):
        sl = slice(hd * LANES, (hd + 1) * LANES)
        k_ref[0, :, sl] = (ka[:, sl] * ck + kb[:, sl] * sk).astype(BF16)
    v_ref[0] = (jnp.dot(kvn16, wv_ref[...], preferred_element_type=F32) + vo_ref[...]).astype(BF16)
    lx_ref[0] = p[:, 512:512 + LRU_WIDTH]
    lg_ref[0] = _gelu_tanh(p[:, 1024:1024 + LRU_WIDTH]).astype(BF16)


def _input_projection(x, sh, sc, g, wts, tabs, with_q):
    b_, s_, d = x.shape
    tm = min(TOKEN_TILE, s_)
    hw = MLA_HEADS * LANES
    per_batch = sh.shape[0] == b_
    mod_spec = pl.BlockSpec((1, 1, d), (lambda b, i: (b, 0, 0)) if per_batch else (lambda b, i: (0, 0, 0)))

    def const(a):
        return pl.BlockSpec(a.shape, lambda b, i: (0,) * a.ndim)

    tab_spec = pl.BlockSpec((tm, LANES), lambda b, i: (i, 0))
    wide = lambda w: pl.BlockSpec((1, tm, w), lambda b, i: (b, i, 0))
    weights = (wts["w_in"], wts["q_g"], wts["kv_g"], wts["wq"], wts["wq_sw"], wts["wk"], wts["wk_sw"], wts["wv"],
               wts["v_ones"])
    out_shape = [jax.ShapeDtypeStruct((b_, s_, hw), BF16),
                 jax.ShapeDtypeStruct((b_, s_, hw), BF16),
                 jax.ShapeDtypeStruct((b_, s_, LRU_WIDTH), F32),
                 jax.ShapeDtypeStruct((b_, s_, LRU_WIDTH), BF16)]
    out_specs = [wide(hw), wide(hw), wide(LRU_WIDTH), wide(LRU_WIDTH)]
    if with_q:
        out_shape = [jax.ShapeDtypeStruct((b_, s_, hw), BF16)] + out_shape
        out_specs = [wide(hw)] + out_specs
    return pl.pallas_call(
        functools.partial(_inproj_kernel, with_q=with_q),
        out_shape=out_shape,
        grid=(b_, s_ // tm),
        in_specs=[wide(d), mod_spec, mod_spec, const(g)] + [const(w) for w in weights] + [tab_spec] * 4,
        out_specs=out_specs,
        compiler_params=_params("parallel", "parallel"),
        name="input_projection_lat" if with_q else "input_projection_ctx",
    )(x, sh, sc, g, *weights, *tabs)


def _attn_kernel(q_ref, kl_ref, kc_ref, vl_ref, vc_ref, o_ref):
    s_ = kl_ref.shape[1]
    kc = min(KEY_CHUNK, s_)
    half = LANES // 2
    lane = lax.broadcasted_iota(jnp.int32, (q_ref.shape[1], LANES), 1)
    chunks = [(kl_ref, vl_ref, c * kc, kc) for c in range(s_ // kc)] + [(kc_ref, vc_ref, 0, kc_ref.shape[1])]
    items = [(a, ch) for a in range(2) for ch in chunks]

    def scores(item):
        a, (k_ref, _, start, size) = item
        sl = slice(a * LANES, (a + 1) * LANES)
        return _nt_dot(q_ref[0, :, sl], k_ref[0, start:start + size, sl])

    out = None
    s_next = scores(items[0])
    for i, (a, (_, v_ref, start, size)) in enumerate(items):
        s = s_next
        if i + 1 < len(items):
            s_next = scores(items[i + 1])
        first = i % len(chunks) == 0
        m_c = jnp.max(s, axis=-1, keepdims=True)
        m_new = m_c if first else jnp.maximum(m, m_c)
        p = jnp.exp(s - m_new).astype(BF16)
        pv = jnp.dot(p, v_ref[0, start:start + size, a * LANES:(a + 1) * LANES], preferred_element_type=F32)
        acc = pv if first else acc * jnp.exp(m - m_new) + pv
        m = m_new
        if (i + 1) % len(chunks) == 0:
            own = (lane < half) if a == 0 else (lane >= half)
            row_sum = pltpu.roll(acc, half, 1)
            o = jnp.where(own, acc / row_sum, 0.0)
            out = o if out is None else out + o
    o_ref[0] = out.astype(BF16)


def _attention(q, k_lat, k_ctx, v_lat, v_ctx):
    b_, s_, hw = q.shape
    nc = k_ctx.shape[1]
    tq = min(Q_TILE, s_)
    pair = 2 * LANES
    n_pairs = hw // pair
    return pl.pallas_call(
        _attn_kernel,
        out_shape=jax.ShapeDtypeStruct((b_, s_, n_pairs * LANES), BF16),
        grid=(b_, n_pairs, s_ // tq),
        in_specs=[pl.BlockSpec((1, tq, pair), lambda b, j, i: (b, i, j)),
                  pl.BlockSpec((1, s_, pair), lambda b, j, i: (b, 0, j)),
                  pl.BlockSpec((1, nc, pair), lambda b, j, i: (b, 0, j)),
                  pl.BlockSpec((1, s_, pair), lambda b, j, i: (b, 0, j)),
                  pl.BlockSpec((1, nc, pair), lambda b, j, i: (b, 0, j))],
        out_specs=pl.BlockSpec((1, tq, LANES), lambda b, j, i: (b, i, j)),
        compiler_params=_params("parallel", "parallel", "arbitrary"),
        name="attention",
    )(q, k_lat, k_ctx, v_lat, v_ctx)


def _scan_groups(a, b, reverse):
    a = a.reshape(a.shape[0] // SUBLANES, SUBLANES, a.shape[1])
    b = b.reshape(a.shape)
    row = lax.broadcasted_iota(jnp.int32, a.shape, 1)
    s = 1
    while s < SUBLANES:
        if reverse:
            keep = row < SUBLANES - s
            a_sh = jnp.where(keep, pltpu.roll(a, SUBLANES - s, 1), 1.0)
            b_sh = jnp.where(keep, pltpu.roll(b, SUBLANES - s, 1), 0.0)
        else:
            keep = row >= s
            a_sh = jnp.where(keep, pltpu.roll(a, s, 1), 1.0)
            b_sh = jnp.where(keep, pltpu.roll(b, s, 1), 0.0)
        b = a * b_sh + b
        a = a * a_sh
        s *= 2
    return a, b


def _chain_groups(scanned, carry, reverse):
    a, b = scanned
    n = a.shape[0]
    hs = [None] * n
    for v in (reversed(range(n)) if reverse else range(n)):
        hs[v] = a[v] * carry + b[v]
        carry = hs[v][0:1] if reverse else hs[v][SUBLANES - 1:SUBLANES]
    return jnp.concatenate(hs, axis=0), carry


def _lru_kernel(xl_ref, xc_ref, gl_ref, cw_ref, cb_ref, wg_ref, bg_ref, lam_ref, o_ref,
                padl_ref, padc_ref, hf_ref, hb_ref):
    s_ = xl_ref.shape[1]
    nc = xc_ref.shape[1]
    tc = SCAN_CHUNK
    halo = SUBLANES
    zero_halo = jnp.zeros((halo, LANES), F32)
    padl_ref[0:halo, :] = zero_halo
    padl_ref[halo:halo + s_, :] = xl_ref[0]
    padl_ref[halo + s_:2 * halo + s_, :] = zero_halo
    padc_ref[0:halo, :] = zero_halo
    padc_ref[halo:halo + nc, :] = xc_ref[0]
    padc_ref[halo + nc:2 * halo + nc, :] = zero_halo

    cw = cw_ref[...]
    cb = cb_ref[...]

    def coeffs(pad_ref, j, d):
        base = j * tc + halo - 2
        u = cb
        for k in range(CONV_W):
            u = u + cw[k:k + 1] * pad_ref[pl.ds(base + k, tc), :]
        t = jnp.tanh(jnp.dot(u.astype(BF16), wg_ref[d, 0], preferred_element_type=F32) + bg_ref[d, 0])
        z = -lam_ref[d]
        softplus = jnp.maximum(z, 0.0) + jnp.log(1.0 + jnp.exp(-jnp.abs(z)))
        half_rate = (-0.5 * RG_C) * softplus
        a = jnp.exp(half_rate * t[:, :LANES] + half_rate)
        b = jnp.sqrt(1.0 - a * a) * ((0.5 * t[:, LANES:] + 0.5) * u)
        return a, b

    def local_scan(pad_ref, j, d):
        a, b = coeffs(pad_ref, j, d)
        return _scan_groups(a, b, reverse=(d == 1))

    def apply_carry(scanned, d, carry):
        return _chain_groups(scanned, carry, reverse=(d == 1))

    cf = cb_ = jnp.zeros((1, LANES), F32)
    n_ctx = nc // tc
    for j in range(n_ctx):
        _, cf = apply_carry(local_scan(padc_ref, j, 0), 0, cf)
        _, cb_ = apply_carry(local_scan(padc_ref, n_ctx - 1 - j, 1), 1, cb_)

    n_lat = s_ // tc
    grp = min(SCAN_GROUP, n_lat)

    def body(jj, carry):
        cf, cb_ = carry
        fwd = [jj * grp + g for g in range(grp)]
        bwd = [n_lat - 1 - jj * grp - g for g in range(grp)]
        scans_f = [local_scan(padl_ref, j, 0) for j in fwd]
        scans_b = [local_scan(padl_ref, j, 1) for j in bwd]
        for j, sc in zip(fwd, scans_f):
            h, cf = apply_carry(sc, 0, cf)
            hf_ref[pl.ds(pl.multiple_of(j * tc, tc), tc), :] = h
        for j, sc in zip(bwd, scans_b):
            h, cb_ = apply_carry(sc, 1, cb_)
            hb_ref[pl.ds(pl.multiple_of(j * tc, tc), tc), :] = h
        return cf, cb_

    lax.fori_loop(0, n_lat // grp, body, (cf, cb_))
    o_ref[0] = ((hf_ref[...] + hb_ref[...]) * gl_ref[0].astype(F32)).astype(BF16)


def _rglru(lx_lat, lx_ctx, gl, conv_w, conv_b, wg, bg, lam):
    b_, s_, w = lx_lat.shape
    nc = lx_ctx.shape[1]
    ng = w // LANES
    halo = SUBLANES
    return pl.pallas_call(
        _lru_kernel,
        out_shape=jax.ShapeDtypeStruct((b_, s_, w), BF16),
        grid=(b_, ng),
        in_specs=[pl.BlockSpec((1, s_, LANES), lambda b, g: (b, 0, g)),
                  pl.BlockSpec((1, nc, LANES), lambda b, g: (b, 0, g)),
                  pl.BlockSpec((1, s_, LANES), lambda b, g: (b, 0, g)),
                  pl.BlockSpec((CONV_W, LANES), lambda b, g: (0, g)),
                  pl.BlockSpec((1, LANES), lambda b, g: (0, g)),
                  pl.BlockSpec((2, 1, LANES, 2 * LANES), lambda b, g: (0, g, 0, 0)),
                  pl.BlockSpec((2, 1, 1, 2 * LANES), lambda b, g: (0, g, 0, 0)),
                  pl.BlockSpec((2, 1, LANES), lambda b, g: (0, 0, g))],
        out_specs=pl.BlockSpec((1, s_, LANES), lambda b, g: (b, 0, g)),
        scratch_shapes=[pltpu.VMEM((s_ + 2 * halo, LANES), F32),
                        pltpu.VMEM((nc + 2 * halo, LANES), F32),
                        pltpu.VMEM((s_, LANES), F32),
                        pltpu.VMEM((s_, LANES), F32)],
        compiler_params=_params("parallel", "parallel"),
        name="rglru",
    )(lx_lat, lx_ctx, gl, conv_w, conv_b, wg, bg, lam)


def _first_index(mask, iota, limit, axis):
    return jnp.min(jnp.where(mask, iota, limit), axis=axis, keepdims=True)


def _outproj_kernel(o_ref, y2_ref, x_ref, g1_ref, sh_ref, sc_ref, g2_ref, gf_ref, wo_ref, wrt_ref, rb_ref,
                    wsgu_ref, wsd_ref, xs_ref, fp_ref, e_ref, w_ref, stage_ref):
    half = wo_ref.shape[0] // 2
    hm = x_ref.shape[1] // OUTPROJ_SPLIT
    groups = [slice(h * hm, (h + 1) * hm) for h in range(OUTPROJ_SPLIT)]

    def mix_of(r):
        return (jnp.dot(o_ref[0, r, :], wo_ref[0:half, :], preferred_element_type=F32)
                + jnp.dot(y2_ref[0, r, :], wo_ref[half:, :], preferred_element_type=F32))

    def norm_of(r, mix):
        x1 = x_ref[0, r, :] + g1_ref[0] * mix
        f = _rms(x1, gf_ref[...]) * (1.0 + sc_ref[0]) + sh_ref[0]
        f16 = f.astype(BF16)
        return x1, f, f16, (f - f16.astype(F32)).astype(BF16)

    def project(f16, f_lo):
        both = _nt_dot(wrt_ref[...], f16)
        logits = both[:N_EXPERTS] + both[N_EXPERTS:] + _nt_dot(wrt_ref[0:N_EXPERTS, :], f_lo)
        return logits, jnp.dot(f16, wsgu_ref[...], preferred_element_type=F32)

    def finish(r, x1, f, a):
        ff = a.shape[1] // 2
        act = _silu(a[:, :ff]) * a[:, ff:]
        shared = jnp.dot(act.astype(BF16), wsd_ref[...], preferred_element_type=F32)
        xs_ref[0, r, :] = x1 + g2_ref[0] * shared
        _rows_to_slabs(f, stage_ref, fp_ref.at[0, pl.ds(r.start * SUBLANES, hm * SUBLANES)])

    mixes = [mix_of(r) for r in groups]
    normed = [None] * OUTPROJ_SPLIT
    projected = [None] * OUTPROJ_SPLIT
    normed[0] = norm_of(groups[0], mixes[0])
    for h in range(OUTPROJ_SPLIT):
        projected[h] = project(normed[h][2], normed[h][3])
        if h + 1 < OUTPROJ_SPLIT:
            normed[h + 1] = norm_of(groups[h + 1], mixes[h + 1])
    for h, r in enumerate(groups):
        _route(projected[h][0], rb_ref, e_ref, w_ref, r)
        finish(r, normed[h][0], normed[h][1], projected[h][1])


def _route(logits, rb_ref, e_ref, w_ref, cols):
    tm = logits.shape[1]
    scores = jax.nn.sigmoid(logits)
    sel = scores + rb_ref[...]
    per = N_EXPERTS // N_GROUPS
    g3 = sel.reshape(N_GROUPS, per, tm)
    mem = lax.broadcasted_iota(jnp.int32, g3.shape, 1)
    m1 = jnp.max(g3, axis=1, keepdims=True)
    first = _first_index(g3 == m1, mem, per, 1)
    m2 = jnp.max(jnp.where(mem == first, NEG_INF, g3), axis=1, keepdims=True)
    gscore = (m1 + m2).reshape(N_GROUPS, tm)
    giota = lax.broadcasted_iota(jnp.int32, gscore.shape, 0)
    gmask = jnp.zeros(gscore.shape, F32)
    cur = gscore
    for _ in range(TOPK_GROUPS):
        mx = jnp.max(cur, axis=0, keepdims=True)
        pick = giota == _first_index(cur == mx, giota, N_GROUPS, 0)
        gmask = jnp.where(pick, 1.0, gmask)
        cur = jnp.where(pick, NEG_INF, cur)
    allowed = jnp.broadcast_to(gmask.reshape(N_GROUPS, 1, tm), g3.shape) > 0.0
    cur = jnp.where(allowed, g3, NEG_INF).reshape(N_EXPERTS, tm)
    eiota = lax.broadcasted_iota(jnp.int32, cur.shape, 0)
    picked_w = []
    for k in range(TOP_K):
        mx = jnp.max(cur, axis=0, keepdims=True)
        idx = _first_index(cur == mx, eiota, N_EXPERTS, 0)
        pick = eiota == idx
        e_ref[k:k + 1, cols] = idx
        picked_w.append(jnp.sum(jnp.where(pick, scores, 0.0), axis=0, keepdims=True))
        cur = jnp.where(pick, NEG_INF, cur)
    wsum = picked_w[0]
    for k in range(1, TOP_K):
        wsum = wsum + picked_w[k]
    for k in range(TOP_K):
        w_ref[k:k + 1, cols] = ROUTED_SCALE * picked_w[k] / wsum


def _output_projection(o, y2, x, g1, sh2, sc2, g2, gf, wts):
    b_, s_, d = x.shape
    tm = min(TOKEN_TILE, s_)
    nt = s_ // tm
    mod_spec = pl.BlockSpec((1, 1, d), lambda b, i: (b, 0, 0))

    def const(a):
        return pl.BlockSpec(a.shape, lambda b, i: (0,) * a.ndim)

    wide = lambda w: pl.BlockSpec((1, tm, w), lambda b, i: (b, i, 0))
    route_spec = pl.BlockSpec((TOP_K, tm), lambda b, i: (0, b * nt + i))
    weights = (wts["w_out"], wts["router_t"], wts["router_b"], wts["sh_gu"], wts["sh_d"])
    return pl.pallas_call(
        _outproj_kernel,
        out_shape=[jax.ShapeDtypeStruct((b_, s_, d), F32),
                   jax.ShapeDtypeStruct((b_, s_ * SUBLANES, LANES), F32),
                   jax.ShapeDtypeStruct((TOP_K, b_ * s_), jnp.int32),
                   jax.ShapeDtypeStruct((TOP_K, b_ * s_), F32)],
        grid=(b_, nt),
        in_specs=[wide(o.shape[2]), wide(y2.shape[2]), wide(d), mod_spec, mod_spec, mod_spec, mod_spec,
                  const(gf)] + [const(w) for w in weights],
        out_specs=[wide(d), pl.BlockSpec((1, tm * SUBLANES, LANES), lambda b, i: (b, i, 0)), route_spec, route_spec],
        scratch_shapes=[pltpu.VMEM((SUBLANES * (tm + 1), LANES), F32)],
        compiler_params=_params("parallel", "parallel"),
        name="output_projection",
    )(o, y2, x, g1, sh2, sc2, g2, gf, *weights)


def _moe_kernel(tile_e_ref, tile_src_ref, tile_n_ref, ntiles_ref, tok_ref, wl_ref,
                fp_ref, wgu_ref, wd_ref, acc_ref, tin_a, tin_b, tout_a, tout_b, act_a, act_b):
    t = pl.program_id(0)
    m = EXPERT_TILE
    st = TILE_STRIDE
    n_out = tout_a.shape[0] // st

    def slab(off):
        return pl.ds(pl.multiple_of(off, SUBLANES), SUBLANES)

    def gather(tile, tin):
        base = tile_src_ref[tile]
        for mi in range(m):
            tin[pl.ds(mi, SUBLANES, stride=st), :] = fp_ref[slab(tok_ref[base + mi]), :]

    def up_project(tin, act):
        xt = jnp.concatenate([tin[j * st:j * st + m, :].astype(BF16) for j in range(n_out)], axis=-1)
        hcat = jnp.dot(xt, wgu_ref[0], preferred_element_type=F32)
        ff = hcat.shape[1] // 2
        act[...] = (_silu(hcat[:, :ff]) * hcat[:, ff:]).astype(BF16)

    def down_project(tile, act, tout):
        y = jnp.dot(act[...], wd_ref[0], preferred_element_type=F32)
        valid = lax.broadcasted_iota(jnp.int32, (m, LANES), 0) < tile_n_ref[tile]
        for j in range(n_out):
            tout[j * st:j * st + m, :] = jnp.where(valid, y[:, j * LANES:(j + 1) * LANES], 0.0)

    def scatter(tile, tout):
        base = tile_src_ref[tile]
        for g in range(m // SCATTER_UNROLL):
            pend = []
            for r in range(SCATTER_UNROLL):
                row = g * SCATTER_UNROLL + r
                dst = slab(tok_ref[base + row])
                pend.append((dst, acc_ref[dst, :] + wl_ref[base + row] * tout[pl.ds(row, SUBLANES, stride=st), :]))
            for dst, new in reversed(pend):
                acc_ref[dst, :] = new

    @pl.when(t == 0)
    def _():
        acc_ref[...] = jnp.zeros(acc_ref.shape, F32)
        tout_a[...] = jnp.zeros(tout_a.shape, F32)
        act_b[...] = jnp.zeros(act_b.shape, BF16)
        gather(0, tin_a)

    def step(tin_cur, tin_nxt, act_cur, act_prev, tout_cur, tout_prev):
        gather(t + 1, tin_nxt)
        up_project(tin_cur, act_cur)
        down_project(jnp.maximum(t - 1, 0), act_prev, tout_prev)
        scatter(jnp.maximum(t - 2, 0), tout_cur)

    active = t <= ntiles_ref[0] + 1

    @pl.when(active & (t % 2 == 0))
    def _():
        step(tin_a, tin_b, act_a, act_b, tout_a, tout_b)

    @pl.when(active & (t % 2 == 1))
    def _():
        step(tin_b, tin_a, act_b, act_a, tout_b, tout_a)


def _routed_experts(tables, tok, wl, buf, wgu, wd, b):
    d = wd.shape[2]
    _, rows, _ = buf.shape
    assert d == SUBLANES * LANES
    stage = d // LANES * TILE_STRIDE
    block = pl.BlockSpec((None, rows, LANES), lambda t, *_: (b, 0, 0), pipeline_mode=pl.Buffered(1))
    n_prefetch = len(tables) + 2
    grid_spec = pltpu.PrefetchScalarGridSpec(
        num_scalar_prefetch=n_prefetch,
        grid=(tables[0].shape[0] - 1,),
        in_specs=[block,
                  pl.BlockSpec((1,) + wgu.shape[1:], lambda t, te, *_: (te[t], 0, 0)),
                  pl.BlockSpec((1,) + wd.shape[1:], lambda t, te, *_: (te[jnp.maximum(t - 1, 0)], 0, 0))],
        out_specs=block,
        scratch_shapes=[pltpu.VMEM((stage, LANES), F32) for _ in range(4)]
        + [pltpu.VMEM((EXPERT_TILE, wd.shape[1]), BF16) for _ in range(2)],
    )
    return pl.pallas_call(
        _moe_kernel,
        out_shape=jax.ShapeDtypeStruct(buf.shape, F32),
        grid_spec=grid_spec,
        input_output_aliases={n_prefetch: 0},
        compiler_params=_params("arbitrary"),
        name="routed_experts",
    )(*tables, tok, wl, buf, wgu, wd)


def _final_kernel(xs_ref, r_ref, g2_ref, g_ref, o_ref, stage_ref):
    routed = _slabs_to_rows(r_ref, stage_ref, xs_ref.shape[1])
    o_ref[0] = _rms(xs_ref[0] + g2_ref[0] * routed, g_ref[...])


def _final(xs, routed, g2, g):
    b_, s_, d = xs.shape
    tm = min(FINAL_TILE, s_)
    nt = s_ // tm
    wide = pl.BlockSpec((1, tm, d), lambda b, i: (b, i, 0))
    return pl.pallas_call(
        _final_kernel,
        out_shape=jax.ShapeDtypeStruct((b_, s_, d), F32),
        grid=(b_, nt),
        in_specs=[wide, pl.BlockSpec((tm * SUBLANES, LANES), lambda b, i: (b * nt + i, 0)),
                  pl.BlockSpec((1, 1, d), lambda b, i: (b, 0, 0)),
                  pl.BlockSpec((1, d), lambda b, i: (0, 0))],
        out_specs=wide,
        scratch_shapes=[pltpu.VMEM((SUBLANES * (tm + 1), LANES), F32)],
        compiler_params=_params("parallel", "parallel"),
        name="final_norm",
    )(xs, routed, g2, g)


def _split_bf16(w):
    hi = w.astype(BF16)
    lo = (w - hi.astype(F32)).astype(BF16)
    return jnp.concatenate([hi, lo], axis=0)


def _prep_weights(w_in, q_norm_g, w_q_up, kv_norm_g, w_kv_up, lru_w_a, lru_b_a, lru_w_x, lru_b_x, w_out,
                  router_w, router_bias, exp_w_gate, exp_w_up, exp_w_down, sh_w_gate, sh_w_up, sh_w_down):
    d = w_in.shape[0]
    h_ = MLA_HEADS
    pad_kr = jnp.zeros((d, LANES - MLA_ROPE), F32)
    w_in_p = jnp.concatenate([w_in[:, :COL_LRU_X], pad_kr, w_in[:, COL_LRU_X:]], axis=1).astype(BF16)

    wq = w_q_up.reshape(MLA_Q_RANK, h_, MLA_NOPE + MLA_ROPE)
    nope, rope = wq[:, :, :MLA_NOPE], wq[:, :, MLA_NOPE:]
    rope_sw = rope.reshape(MLA_Q_RANK, h_, 2, 2, ROPE_AXIS // 2)[:, :, :, ::-1, :].reshape(rope.shape)
    zpad = jnp.zeros((MLA_Q_RANK, h_, LANES - MLA_NOPE - MLA_ROPE), F32)
    wq_p = jnp.concatenate([nope, rope, zpad], axis=-1).reshape(MLA_Q_RANK, h_ * LANES).astype(BF16)
    wq_sw = jnp.concatenate([jnp.zeros_like(nope), rope_sw, zpad], axis=-1).reshape(MLA_Q_RANK, h_ * LANES).astype(BF16)

    wkv = w_kv_up.reshape(MLA_KV_RANK, h_, MLA_NOPE + MLA_V)
    k_nope, v_w = wkv[:, :, :MLA_NOPE], wkv[:, :, MLA_NOPE:]
    r_idx = jnp.arange(MLA_ROPE)
    place = jnp.zeros((LANES, h_, LANES), F32).at[r_idx, :, MLA_NOPE + r_idx].set(1.0)
    place_sw = jnp.zeros((LANES, h_, LANES), F32).at[r_idx ^ (ROPE_AXIS // 2), :, MLA_NOPE + r_idx].set(1.0)
    k_top = jnp.concatenate([k_nope, jnp.zeros((MLA_KV_RANK, h_, LANES - MLA_NOPE), F32)], axis=-1)
    wk = jnp.concatenate([k_top, place], axis=0).reshape(MLA_KV_RANK + LANES, h_ * LANES).astype(BF16)
    wk_sw = jnp.concatenate([jnp.zeros_like(k_top), place_sw], axis=0).reshape(MLA_KV_RANK + LANES, h_ * LANES).astype(BF16)
    zv = jnp.zeros_like(v_w)
    even = (jnp.arange(h_) % 2 == 0)[None, :, None]
    wv = jnp.concatenate([jnp.where(even, v_w, zv), jnp.where(even, zv, v_w)], axis=-1)
    wv = wv.reshape(MLA_KV_RANK, h_ * LANES).astype(BF16)
    one_lo = jnp.concatenate([jnp.zeros((MLA_V,), F32), jnp.ones((LANES - MLA_V,), F32)])
    one_hi = jnp.concatenate([jnp.ones((LANES - MLA_V,), F32), jnp.zeros((MLA_V,), F32)])
    v_ones = jnp.where(even[0], one_lo[None, :], one_hi[None, :]).reshape(1, h_ * LANES)

    eye = jnp.eye(LRU_BLOCKS, dtype=F32)
    ng = LRU_WIDTH // LANES

    def dense(w):
        return jnp.einsum("xncd,nm->xncmd", w, eye).reshape(2, LRU_WIDTH, LRU_WIDTH)

    def grp(wd_):
        return jnp.stack([wd_[:, g * LANES:(g + 1) * LANES, g * LANES:(g + 1) * LANES] for g in range(ng)], axis=1)

    wg = (0.5 * jnp.concatenate([grp(dense(lru_w_a)), grp(dense(lru_w_x))], axis=-1)).astype(BF16)
    bg = 0.5 * jnp.concatenate([lru_b_a.reshape(2, ng, 1, LANES), lru_b_x.reshape(2, ng, 1, LANES)], axis=-1)

    return dict(
        w_in=w_in_p, q_g=q_norm_g[None], kv_g=kv_norm_g[None], wq=wq_p, wq_sw=wq_sw, wk=wk, wk_sw=wk_sw, wv=wv,
        v_ones=v_ones,
        wg=wg, bg=bg,
        w_out=w_out.astype(BF16), router_t=_split_bf16(router_w.T), router_b=router_bias[:, None],
        sh_gu=jnp.concatenate([sh_w_gate, sh_w_up], axis=1).astype(BF16), sh_d=sh_w_down.astype(BF16),
        exp_gu=jnp.concatenate([exp_w_gate, exp_w_up], axis=2).astype(BF16), exp_d=exp_w_down.astype(BF16),
    )


def _rope_tables(s_, nc):
    rows = s_ // GRID_W
    row = jnp.repeat(jnp.arange(rows, dtype=F32), GRID_W)
    col = jnp.tile(jnp.arange(GRID_W, dtype=F32), rows)
    inv_freq = ROPE_BASE ** (-jnp.arange(0, ROPE_AXIS, 2, dtype=F32) / ROPE_AXIS)
    ang = jnp.stack([row, col], axis=-1)[:, :, None] * inv_freq
    cos = jnp.broadcast_to(jnp.cos(ang)[:, :, None, :], (s_, 2, 2, ROPE_AXIS // 2)).reshape(s_, MLA_ROPE)
    sin = jnp.sin(ang)[:, :, None, :] * jnp.array([-1.0, 1.0], F32)[None, None, :, None]
    sin = sin.reshape(s_, MLA_ROPE)
    ones = jnp.ones((s_, MLA_NOPE), F32)
    zeros = jnp.zeros((s_, LANES - MLA_NOPE - MLA_ROPE), F32)
    c_tab = jnp.concatenate([ones, cos, zeros], axis=1)
    s_tab = jnp.concatenate([jnp.zeros_like(ones), sin, zeros], axis=1)
    c_ctx = jnp.concatenate([jnp.ones((nc, MLA_NOPE + MLA_ROPE), F32),
                             jnp.zeros((nc, LANES - MLA_NOPE - MLA_ROPE), F32)], axis=1)
    return c_tab, s_tab, c_ctx, jnp.zeros_like(c_ctx)


def _dispatch(eidx, wts, b_, s_):
    n = TOP_K * s_
    e = eidx.reshape(TOP_K, b_, s_).transpose(1, 0, 2).reshape(b_, n)
    w = wts.reshape(TOP_K, b_, s_).transpose(1, 0, 2).reshape(b_, n)
    key = e * n + jnp.arange(n, dtype=jnp.int32)[None, :]
    key_sorted, w_sorted = lax.sort((key, w), dimension=1, num_keys=1)
    tok = ((key_sorted % n) % s_) * SUBLANES
    tok = jnp.pad(tok.astype(jnp.int32), ((0, 0), (0, EXPERT_TILE)))
    w_sorted = jnp.pad(w_sorted, ((0, 0), (0, EXPERT_TILE)))

    ids = jnp.arange(N_EXPERTS, dtype=jnp.int32)
    counts = jnp.sum(e[:, :, None] == ids[None, None, :], axis=1, dtype=jnp.int32)
    starts = jnp.cumsum(counts, axis=1) - counts
    tiles = (counts + EXPERT_TILE - 1) // EXPERT_TILE
    tile_end = jnp.cumsum(tiles, axis=1)
    ntiles = tile_end[:, -1:]
    tid = jnp.arange(n // EXPERT_TILE + N_EXPERTS + 3, dtype=jnp.int32)[None, :]
    live = tid < ntiles
    tile_e = jnp.sum(tid[:, :, None] >= tile_end[:, None, :], axis=-1, dtype=jnp.int32)
    last_e = jnp.max(jnp.where(tiles > 0, ids[None, :], 0), axis=1, keepdims=True)
    tile_e = jnp.where(live, tile_e, last_e)
    take = lambda a: jnp.take_along_axis(a, tile_e, axis=1)
    within = (tid - (take(tile_end) - take(tiles))) * EXPERT_TILE
    tile_src = jnp.where(live, take(starts) + within, 0)
    tile_n = jnp.where(live, jnp.clip(take(counts) - within, 0, EXPERT_TILE), 0)
    return (tile_e, tile_src, tile_n, ntiles), tok, w_sorted


def kernel(x, c, ctx, c_ctx, w_mod, b_mod, norm_mix_g, w_in, q_norm_g, w_q_up, kv_norm_g, w_kv_up, conv_w, conv_b,
           lru_w_a, lru_b_a, lru_w_x, lru_b_x, lru_lambda, w_out, norm_ffn_g, router_w, router_bias, exp_w_gate,
           exp_w_up, exp_w_down, sh_w_gate, sh_w_up, sh_w_down, final_norm_g):
    b_, s_, d = x.shape
    nc = ctx.shape[1]
    assert w_mod.shape[0] == 1, "single-layer operation"
    assert s_ % GRID_W == 0 and s_ % SCAN_CHUNK == 0 and nc % SCAN_CHUNK == 0
    assert s_ % min(TOKEN_TILE, s_) == 0 and s_ % min(Q_TILE, s_) == 0

    wts = _prep_weights(w_in[0], q_norm_g[0], w_q_up[0], kv_norm_g[0], w_kv_up[0], lru_w_a[0], lru_b_a[0],
                        lru_w_x[0], lru_b_x[0], w_out[0], router_w[0], router_bias[0], exp_w_gate[0], exp_w_up[0],
                        exp_w_down[0], sh_w_gate[0], sh_w_up[0], sh_w_down[0])
    c_tab, s_tab, c_ctx_tab, s_ctx_tab = _rope_tables(s_, nc)

    rows = (b_ + 1 + SUBLANES - 1) // SUBLANES * SUBLANES
    cc = jnp.zeros((rows, d), F32).at[:b_].set(c).at[b_].set(c_ctx)
    mod = _modulation(cc, w_mod[0], b_mod[0][None])
    sh1, sc1, g1, sh2, sc2, g2 = [mod[:b_, i * d:(i + 1) * d].reshape(b_, 1, d) for i in range(6)]
    csh1, csc1 = [mod[b_, i * d:(i + 1) * d].reshape(1, 1, d) for i in range(2)]

    q, k_lat, v_lat, lx_lat, gl = _input_projection(
        x, sh1, sc1, norm_mix_g, wts, (c_tab * ATTN_SCALE, s_tab * ATTN_SCALE, c_tab, s_tab), True)
    k_ctx, v_ctx, lx_ctx, _ = _input_projection(
        ctx, csh1, csc1, norm_mix_g, wts, (c_ctx_tab, s_ctx_tab, c_ctx_tab, s_ctx_tab), False)

    o_lat = _attention(q, k_lat, k_ctx, v_lat, v_ctx)
    y2 = _rglru(lx_lat, lx_ctx, gl, conv_w[0], conv_b[0][None], wts["wg"], wts["bg"],
                lru_lambda[0].reshape(2, 1, LRU_WIDTH))

    xs, fp, eidx, rw = _output_projection(o_lat, y2, x, g1, sh2, sc2, g2, norm_ffn_g, wts)

    tables, tok, wl = _dispatch(eidx, rw, b_, s_)
    buf = fp
    for b in range(b_):
        buf = _routed_experts([tb[b] for tb in tables], tok[b], wl[b], buf, wts["exp_gu"], wts["exp_d"], b)
    routed = buf.reshape(b_ * s_ * SUBLANES, LANES)

    return _final(xs, routed, g2, final_norm_g[None])
```
